```python
import math
import jax, jax.numpy as jnp
from jax import lax
import numpy as np

D_MODEL = 1024
BATCH = 2
SEQ = 16384
DEPTH = 1
DEC_BATCH = 128
DEC_SEQ = 8
PAST_LEN = 8192
PAGE_SIZE = 128

H_M = 4
D_HM = D_MODEL // (2 * H_M)
W_M = H_M * D_HM
H_A = 8
D_HA = D_MODEL // (2 * H_A)
W_A = H_A * D_HA
H_I = 4
D_I = 64
TOPK_MAX = 256
N_BUCKETS = 32
MAX_DISTANCE = 128
N_EXPERTS = 32
TOP_K_EXP = 4
D_FF = D_MODEL
SWIGLU_LIMIT = 7.0
SWIGLU_ALPHA = 1.702
GATE_CAP = 15.0
MLSTM_CHUNK = 64
QUERY_BLOCK = 128
EXPERT_BLOCK = 128
LN_EPS = 1e-5
DN_ALPHA = (2.0 * DEPTH) ** 0.25
DN_BETA = (8.0 * DEPTH) ** -0.25
P_IN = 4 * W_M + 2 * H_M + 3 * W_A + H_I * D_I + D_I + H_I

kernel_name = 'hybrid_mlstm_dsa_moe_step'


def _split_sizes():
    return (W_M, W_M, W_M, W_M, H_M, H_M, W_A, W_A, W_A, H_I * D_I, D_I, H_I)


def layer_norm(x, g, b):
    xf = x.astype(jnp.float32)
    mu = jnp.mean(xf, axis=-1, keepdims=True)
    var = jnp.mean(jnp.square(xf - mu), axis=-1, keepdims=True)
    y = (xf - mu) * lax.rsqrt(var + LN_EPS) * g.astype(jnp.float32) + b.astype(jnp.float32)
    return y.astype(x.dtype)


def project(x, w_in):
    B, T, _ = x.shape
    proj = jnp.einsum('btd,dp->btp', x, w_in)
    cuts = np.cumsum(_split_sizes())[:-1].tolist()
    mq, mk, mv, mo, mi, mf, aq, ak, av, iq, ik, iw = jnp.split(proj, cuts, axis=-1)
    return (mq.reshape(B, T, H_M, D_HM), mk.reshape(B, T, H_M, D_HM), mv.reshape(B, T, H_M, D_HM),
            mo, mi, mf,
            aq.reshape(B, T, H_A, D_HA), ak.reshape(B, T, H_A, D_HA), av.reshape(B, T, H_A, D_HA),
            iq.reshape(B, T, H_I, D_I), ik, iw)


def mlstm_chunkwise(q, k, v, ig, lf, C0, n0, m0):
    B, T, H, D = q.shape
    L = math.gcd(T, MLSTM_CHUNK)
    NC = T // L

    def chunks4(a):
        return a.reshape(B, NC, L, H, D).transpose(1, 0, 3, 2, 4)

    def chunks3(a):
        return a.reshape(B, NC, L, H).transpose(1, 0, 3, 2)

    tril = jnp.tril(jnp.ones((L, L), dtype=bool))

    def step(carry, inp):
        C, n, m = carry
        qc, kc, vc, ic, fc = inp
        b = jnp.cumsum(fc, axis=-1)
        dmat = jnp.where(tril, b[..., :, None] - b[..., None, :] + ic[..., None, :], -jnp.inf)
        inter = b + m[..., None]
        mt = jnp.maximum(inter, jnp.max(dmat, axis=-1))
        w_intra = jnp.exp(dmat - mt[..., None])
        w_inter = jnp.exp(inter - mt)
        s = jnp.einsum('bhtd,bhsd->bhts', qc, kc) * w_intra
        num = jnp.einsum('bhts,bhsd->bhtd', s, vc) + w_inter[..., None] * jnp.einsum('bhvk,bhtk->bhtv', C, qc)
        den = jnp.sum(s, axis=-1) + w_inter * jnp.einsum('bhk,bhtk->bht', n, qc)
        h = num / jnp.maximum(jnp.abs(den), jnp.exp(-mt))[..., None]
        bl = b[..., -1]
        decay_s = bl[..., None] - b + ic
        m_new = jnp.maximum(bl + m, jnp.max(decay_s, axis=-1))
        ws = jnp.exp(decay_s - m_new[..., None])
        carry_scale = jnp.exp(bl + m - m_new)
        C_new = carry_scale[..., None, None] * C + jnp.einsum('bhsv,bhsk->bhvk', vc * ws[..., None], kc)
        n_new = carry_scale[..., None] * n + jnp.einsum('bhs,bhsk->bhk', ws, kc)
        return (C_new, n_new, m_new), h

    (C, n, m), hs = lax.scan(step, (C0, n0, m0),
                             (chunks4(q), chunks4(k), chunks4(v), chunks3(ig), chunks3(lf)))
    h = hs.transpose(1, 0, 3, 2, 4).reshape(B, T, H, D)
    return h, C, n, m


def mlstm_branch(mq, mk, mv, mo, mi, mf, b_ig, b_fg, norm_g, C0, n0, m0):
    f32 = jnp.float32
    B, T = mq.shape[:2]
    ig = GATE_CAP * jnp.tanh((mi.astype(f32) + b_ig.astype(f32)) / GATE_CAP)
    lf = jax.nn.log_sigmoid(GATE_CAP * jnp.tanh((mf.astype(f32) + b_fg.astype(f32)) / GATE_CAP))
    h, C, n, m = mlstm_chunkwise(mq.astype(f32), mk.astype(f32) * (D_HM ** -0.5), mv.astype(f32),
                                 ig, lf, C0.astype(f32), n0.astype(f32), m0.astype(f32))
    mu = jnp.mean(h, axis=-1, keepdims=True)
    var = jnp.mean(jnp.square(h - mu), axis=-1, keepdims=True)
    hn = ((h - mu) * lax.rsqrt(var + LN_EPS)).reshape(B, T, W_M) * norm_g.astype(f32)
    out = jax.nn.sigmoid(mo.astype(f32)) * hn
    return out.astype(mq.dtype), C, n, m


def rel_bucket(dist):
    max_exact = N_BUCKETS // 2
    d = jnp.maximum(dist, 0)
    df = jnp.maximum(d.astype(jnp.float32), 1.0)
    large = max_exact + (jnp.log(df / max_exact) / math.log(MAX_DISTANCE / max_exact)
                         * (N_BUCKETS - max_exact)).astype(jnp.int32)
    large = jnp.minimum(large, N_BUCKETS - 1)
    return jnp.where(d < max_exact, d, large)


def indexer_select(iq, iw, ki, qpos, kpos, topk):
    s = jax.nn.relu(jnp.einsum('bqhd,bld->bqhl', iq, ki))
    w = iw * (H_I ** -0.5 * D_I ** -0.5)
    score = jnp.einsum('bqhl,bqh->bql', s, w).astype(jnp.float32)
    score = jnp.where(kpos[None, None, :] <= qpos[None, :, None], score, -jnp.inf)
    _, idx = lax.top_k(score, topk)
    return idx


def gather_rows(a, idx):
    return jax.vmap(lambda ab, ib: ab[ib])(a, idx)


def sparse_attend(q, k_sel, v_sel, qpos, kpos_sel, rel_bias):
    logits = jnp.einsum('bqhd,bqkhd->bqkh', q, k_sel).astype(jnp.float32) * (D_HA ** -0.5)
    dist = qpos[None, :, None] - kpos_sel
    bias = rel_bias[rel_bucket(dist)].astype(jnp.float32)
    logits = jnp.where((dist >= 0)[..., None], logits + bias, -jnp.inf)
    p = jax.nn.softmax(logits, axis=2)
    return jnp.einsum('bqkh,bqkhd->bqhd', p.astype(v_sel.dtype), v_sel).astype(q.dtype)


def dsa_prompt(aq, ak, av, iq, iw, ik, rel_bias):
    B, S = aq.shape[:2]
    topk = min(TOPK_MAX, S // 4)
    nb = S // QUERY_BLOCK
    kpos = jnp.arange(S, dtype=jnp.int32)

    def blocks(a):
        return a.reshape(B, nb, QUERY_BLOCK, *a.shape[2:]).swapaxes(0, 1)

    def body(inp):
        qb, iqb, iwb, qpos = inp
        idx = indexer_select(iqb, iwb, ik, qpos, kpos, topk)
        return sparse_attend(qb, gather_rows(ak, idx), gather_rows(av, idx), qpos, idx, rel_bias)

    out = lax.map(body, (blocks(aq), blocks(iq), blocks(iw), kpos.reshape(nb, QUERY_BLOCK)))
    return out.swapaxes(0, 1).reshape(B, S, W_A)


def dsa_sample(aq, ak, av, iq, iw, ik, cache_k, cache_v, cache_k_idx, layer, page_table, rel_bias):
    DB, T = aq.shape[:2]
    n_pages = page_table.shape[1]
    page = cache_k.shape[2]
    past = n_pages * page
    n_keys = past + T
    topk = min(TOPK_MAX, n_keys // 4)
    ki_past = cache_k_idx[layer, page_table].reshape(DB, past, D_I)
    ki_all = jnp.concatenate([ki_past, ik.astype(ki_past.dtype)], axis=1)
    qpos = past + jnp.arange(T, dtype=jnp.int32)
    kpos = jnp.arange(n_keys, dtype=jnp.int32)
    idx = indexer_select(iq, iw, ki_all, qpos, kpos, topk)
    in_past = (idx < past)[..., None, None]
    pidx = jnp.minimum(idx, past - 1)
    phys = jnp.take_along_axis(page_table, (pidx // page).reshape(DB, -1), axis=1).reshape(idx.shape)
    off = pidx % page
    nidx = jnp.clip(idx - past, 0, T - 1)
    k_sel = jnp.where(in_past, cache_k[layer, phys, off], gather_rows(ak, nidx).astype(cache_k.dtype))
    v_sel = jnp.where(in_past, cache_v[layer, phys, off], gather_rows(av, nidx).astype(cache_v.dtype))
    out = sparse_attend(aq, k_sel.astype(aq.dtype), v_sel.astype(aq.dtype), qpos, idx, rel_bias)
    return out.reshape(DB, T, W_A)


def moe(x, router_w, router_b, w_gu, b_gu, w_dn, b_dn):
    B, T, D = x.shape
    xt = x.reshape(-1, D)
    n_tok = xt.shape[0]
    logits = (xt @ router_w + router_b).astype(jnp.float32)
    top_v, top_i = lax.top_k(logits, TOP_K_EXP)
    gate = jax.nn.softmax(top_v, axis=-1)
    n_asg = n_tok * TOP_K_EXP
    e_flat = top_i.reshape(-1)
    tok_flat = jnp.arange(n_asg, dtype=jnp.int32) // TOP_K_EXP
    g_flat = gate.reshape(-1)
    order = jnp.argsort(e_flat)
    e_s, tok_s, g_s = e_flat[order], tok_flat[order], g_flat[order]
    counts = jnp.bincount(e_flat, length=N_EXPERTS)
    padded = (counts + EXPERT_BLOCK - 1) // EXPERT_BLOCK * EXPERT_BLOCK
    pad_end = jnp.cumsum(padded)
    pad_start = pad_end - padded
    start = jnp.cumsum(counts) - counts
    dest = pad_start[e_s] + jnp.arange(n_asg, dtype=jnp.int32) - start[e_s]
    n_blk = -(-n_asg // EXPERT_BLOCK) + N_EXPERTS
    buf_tok = jnp.full((n_blk * EXPERT_BLOCK,), n_tok, dtype=jnp.int32).at[dest].set(tok_s)
    buf_g = jnp.zeros((n_blk * EXPERT_BLOCK,), jnp.float32).at[dest].set(g_s)
    blk_e = jnp.minimum(jnp.searchsorted(pad_end, jnp.arange(n_blk) * EXPERT_BLOCK, side='right'),
                        N_EXPERTS - 1)
    x_pad = jnp.concatenate([xt, jnp.zeros((1, D), xt.dtype)], axis=0)

    def expert_block(inp):
        tok, g, e = inp
        gu = x_pad[tok] @ w_gu[e] + b_gu[e]
        gt, up = jnp.split(gu, 2, axis=-1)
        gt = jnp.minimum(gt, SWIGLU_LIMIT)
        up = jnp.clip(up, -SWIGLU_LIMIT, SWIGLU_LIMIT)
        act = (up + 1.0) * gt * jax.nn.sigmoid(SWIGLU_ALPHA * gt)
        yb = act @ w_dn[e] + b_dn[e]
        return yb * g[:, None].astype(yb.dtype)

    yb = lax.map(expert_block, (buf_tok.reshape(n_blk, EXPERT_BLOCK), buf_g.reshape(n_blk, EXPERT_BLOCK), blk_e))
    y = jnp.zeros((n_tok + 1, D), x.dtype).at[buf_tok].add(yb.reshape(-1, D).astype(x.dtype))[:n_tok]
    return y.reshape(B, T, D)


def finish_layer(x, m_out, a_out, w_out, ln1_g, ln1_b, router_w, router_b, w_gu, b_gu, w_dn, b_dn, ln2_g, ln2_b):
    mix = jnp.einsum('btc,cd->btd', jnp.concatenate([m_out, a_out.astype(m_out.dtype)], axis=-1), w_out)
    h = layer_norm(DN_ALPHA * x + mix, ln1_g, ln1_b)
    return layer_norm(DN_ALPHA * h + moe(h, router_w, router_b, w_gu, b_gu, w_dn, b_dn), ln2_g, ln2_b)


def setup_inputs(seed: int = 0) -> dict:
    key = jax.random.key(seed)
    ks = jax.random.split(key, 26)
    f32 = jnp.float32
    nrm = jax.random.normal
    n_pages = PAST_LEN // PAGE_SIZE
    n_used = DEC_BATCH * n_pages
    n_pool = n_used + max(1, n_used // 4)
    col_scale = np.ones((P_IN,), np.float32)
    col_scale[2 * W_M:3 * W_M] = DN_BETA
    a_v0 = 4 * W_M + 2 * H_M + 2 * W_A
    col_scale[a_v0:a_v0 + W_A] = DN_BETA
    page_table = jax.random.permutation(ks[8], n_pool)[:n_used].reshape(DEC_BATCH, n_pages).astype(jnp.int32)
    return {
        'x_prompt': nrm(ks[0], (BATCH, SEQ, D_MODEL), f32),
        'x_sample': nrm(ks[1], (DEC_BATCH, DEC_SEQ, D_MODEL), f32),
        'cache_k': nrm(ks[2], (DEPTH, n_pool, PAGE_SIZE, H_A, D_HA), f32),
        'cache_v': nrm(ks[3], (DEPTH, n_pool, PAGE_SIZE, H_A, D_HA), f32),
        'cache_k_idx': nrm(ks[4], (DEPTH, n_pool, PAGE_SIZE, D_I), f32),
        'state_C': 0.1 * nrm(ks[5], (DEPTH, DEC_BATCH, H_M, D_HM, D_HM), f32),
        'state_n': nrm(ks[6], (DEPTH, DEC_BATCH, H_M, D_HM), f32),
        'state_m': nrm(ks[7], (DEPTH, DEC_BATCH, H_M), f32),
        'page_table': page_table,
        'w_in': nrm(ks[9], (DEPTH, D_MODEL, P_IN), f32) * (D_MODEL ** -0.5) * jnp.asarray(col_scale),
        'b_ig': 0.1 * nrm(ks[10], (DEPTH, H_M), f32),
        'b_fg': 3.0 + jnp.linspace(0.0, 3.0, H_M, dtype=f32) + 0.1 * nrm(ks[11], (DEPTH, H_M), f32),
        'mlstm_norm_g': 1.0 + 0.02 * nrm(ks[12], (DEPTH, W_M), f32),
        'w_out': nrm(ks[13], (DEPTH, D_MODEL, D_MODEL), f32) * (D_MODEL ** -0.5) * DN_BETA,
        'ln1_g': 1.0 + 0.02 * nrm(ks[14], (DEPTH, D_MODEL), f32),
        'ln1_b': 0.02 * nrm(ks[15], (DEPTH, D_MODEL), f32),
        'router_w': nrm(ks[16], (DEPTH, D_MODEL, N_EXPERTS), f32) * (D_MODEL ** -0.5),
        'router_b': 0.01 * nrm(ks[17], (DEPTH, N_EXPERTS), f32),
        'w_gate_up': nrm(ks[18], (DEPTH, N_EXPERTS, D_MODEL, 2 * D_FF), f32) * (D_MODEL ** -0.5) * DN_BETA,
        'b_gate_up': 0.02 * nrm(ks[19], (DEPTH, N_EXPERTS, 2 * D_FF), f32),
        'w_down': nrm(ks[20], (DEPTH, N_EXPERTS, D_FF, D_MODEL), f32) * (D_FF ** -0.5) * DN_BETA,
        'b_down': 0.02 * nrm(ks[21], (DEPTH, N_EXPERTS, D_MODEL), f32),
        'ln2_g': 1.0 + 0.02 * nrm(ks[22], (DEPTH, D_MODEL), f32),
        'ln2_b': 0.02 * nrm(ks[23], (DEPTH, D_MODEL), f32),
        'rel_bias': 0.1 * nrm(ks[24], (N_BUCKETS, H_A), f32),
    }


def reference(x_prompt, x_sample, cache_k, cache_v, cache_k_idx, state_C, state_n, state_m, page_table,
              w_in, b_ig, b_fg, mlstm_norm_g, w_out, ln1_g, ln1_b, router_w, router_b,
              w_gate_up, b_gate_up, w_down, b_down, ln2_g, ln2_b, rel_bias):
    Bp = x_prompt.shape[0]
    yp, ys = x_prompt, x_sample
    kp_l, vp_l, kip_l, Cp_l, np_l, mp_l = [], [], [], [], [], []
    ks_l, vs_l, kis_l, Cs_l, ns_l, ms_l = [], [], [], [], [], []
    for l in range(DEPTH):
        ffn = (w_out[l], ln1_g[l], ln1_b[l], router_w[l], router_b[l], w_gate_up[l], b_gate_up[l],
               w_down[l], b_down[l], ln2_g[l], ln2_b[l])
        mq, mk, mv, mo, mi, mf, aq, ak, av, iq, ik, iw = project(yp, w_in[l])
        C0 = jnp.zeros((Bp, H_M, D_HM, D_HM), jnp.float32)
        n0 = jnp.zeros((Bp, H_M, D_HM), jnp.float32)
        m0 = jnp.zeros((Bp, H_M), jnp.float32)
        m_out, C_p, n_p, m_p = mlstm_branch(mq, mk, mv, mo, mi, mf, b_ig[l], b_fg[l], mlstm_norm_g[l], C0, n0, m0)
        a_out = dsa_prompt(aq, ak, av, iq, iw, ik, rel_bias)
        kp_l.append(ak); vp_l.append(av); kip_l.append(ik)
        Cp_l.append(C_p.astype(yp.dtype)); np_l.append(n_p.astype(yp.dtype)); mp_l.append(m_p.astype(yp.dtype))
        yp = finish_layer(yp, m_out, a_out, *ffn)
        mq, mk, mv, mo, mi, mf, aq, ak, av, iq, ik, iw = project(ys, w_in[l])
        m_out, C_s, n_s, m_s = mlstm_branch(mq, mk, mv, mo, mi, mf, b_ig[l], b_fg[l], mlstm_norm_g[l],
                                            state_C[l], state_n[l], state_m[l])
        a_out = dsa_sample(aq, ak, av, iq, iw, ik, cache_k, cache_v, cache_k_idx, l, page_table, rel_bias)
        ks_l.append(ak); vs_l.append(av); kis_l.append(ik)
        Cs_l.append(C_s.astype(state_C.dtype)); ns_l.append(n_s.astype(state_n.dtype)); ms_l.append(m_s.astype(state_m.dtype))
        ys = finish_layer(ys, m_out, a_out, *ffn)
    return (yp, ys,
            jnp.stack(kp_l), jnp.stack(vp_l), jnp.stack(kip_l),
            jnp.stack(Cp_l), jnp.stack(np_l), jnp.stack(mp_l),
            jnp.stack(ks_l), jnp.stack(vs_l), jnp.stack(kis_l),
            jnp.stack(Cs_l), jnp.stack(ns_l), jnp.stack(ms_l))
```

```python
import functools
import math

import numpy as np
import jax
import jax.numpy as jnp
from jax import lax
from jax.experimental import pallas as pl
from jax.experimental.pallas import tpu as pltpu

F32, BF16, I32 = jnp.float32, jnp.bfloat16, jnp.int32

H_M, D_HM = 4, 128
H_A, D_HA = 8, 64
H_I, D_I = 4, 64
W_M, W_A = H_M * D_HM, H_A * D_HA
TOPK_MAX = 256
N_BUCKETS, MAX_DISTANCE = 32, 128
N_EXPERTS, TOP_K_EXP = 32, 4
SWIGLU_LIMIT, SWIGLU_ALPHA = 7.0, 1.702
GATE_CAP = 15.0
LN_EPS = 1e-5
DEPTH = 1
DN_ALPHA = (2.0 * DEPTH) ** 0.25

LANES = 128
VMEM_LIMIT_BYTES = 56 * 1024 * 1024
NEG = -1e30
NEG_BITS = int(np.float32(NEG).view(np.int32))
INT_MIN = -2 ** 31
KEY_CHUNK = 512
TAIL = KEY_CHUNK + LANES
ROW_BLOCK = 512
SMALL_IK, SMALL_MI, SMALL_MF, SMALL_IW = 0, 64, 68, 72

_NT = (((1,), (1,)), ((), ()))


def _cparams(sem):
    return pltpu.CompilerParams(dimension_semantics=sem, vmem_limit_bytes=VMEM_LIMIT_BYTES)


def _const_spec(shape):
    zeros = (0,) * len(shape)
    return pl.BlockSpec(shape, lambda *_: zeros, pipeline_mode=pl.Buffered(1))


_PROJ_COMMON = ("mq", "mk", "mv", "mo", "sm", "aq", "qs", "lo", "hi")
_PROJ_EXTRA = {
    True: ("avb", "ktb", "iktb", "ktf", "vtf", "iktf"),
    False: ("ak", "av", "ik"),
}


def _proj_kernel(x_ref, wm_ref, wa_ref, wi_ref, ws_ref, wt_ref, *out_refs, transposed):
    o = dict(zip(_PROJ_COMMON + _PROJ_EXTRA[transposed], out_refs))
    xb = x_ref[...].astype(BF16)
    ym = jnp.dot(xb, wm_ref[...], preferred_element_type=F32)
    o["mq"][...] = ym[:, 0:W_M].astype(BF16)
    o["mk"][...] = (ym[:, W_M:2 * W_M] * (D_HM ** -0.5)).astype(BF16)
    o["mv"][...] = ym[:, 2 * W_M:3 * W_M].astype(BF16)
    o["mo"][...] = ym[:, 3 * W_M:4 * W_M]
    ya = jnp.dot(xb, wa_ref[...], preferred_element_type=F32)
    o["aq"][...] = (ya[:, 0:W_A] * (D_HA ** -0.5)).astype(BF16)
    xl = (x_ref[...] - xb.astype(F32)).astype(BF16)
    ys = (jnp.dot(xb, ws_ref[0], preferred_element_type=F32) + jnp.dot(xl, ws_ref[0], preferred_element_type=F32)
          + jnp.dot(xb, ws_ref[1], preferred_element_type=F32))
    o["sm"][...] = ys
    yi = jnp.dot(xb, wi_ref[...], preferred_element_type=F32)
    wsc = ys * (H_I ** -0.5 * D_I ** -0.5)
    lane = lax.broadcasted_iota(I32, yi.shape, 1)
    scale = wsc[:, SMALL_IW + 3:SMALL_IW + 4]
    for h in (2, 1, 0):
        scale = jnp.where(lane < (h + 1) * D_I, wsc[:, SMALL_IW + h:SMALL_IW + h + 1], scale)
    o["qs"][...] = (yi * scale).astype(BF16)
    o["lo"][...] = jnp.where(ys >= 0.0, 0.0, -jnp.inf)
    o["hi"][...] = jnp.where(ys <= 0.0, 0.0, jnp.inf)
    if transposed:
        o["avb"][...] = ya[:, W_A:2 * W_A].astype(BF16)
        yt = lax.dot_general(wt_ref[...], xb, _NT, preferred_element_type=F32)
        o["ktf"][...] = yt[0:W_A]
        o["ktb"][...] = yt[0:W_A].astype(BF16)
        o["vtf"][...] = yt[W_A:2 * W_A]
        o["iktf"][...] = yt[2 * W_A:2 * W_A + D_I]
        o["iktb"][...] = yt[2 * W_A:2 * W_A + D_I].astype(BF16)
    else:
        o["ak"][...] = ya[:, W_A:2 * W_A]
        o["av"][...] = ya[:, 2 * W_A:3 * W_A]
        o["ik"][...] = ys[:, SMALL_IK:SMALL_IK + D_I]


def _pack_w_in(w_in, transposed):
    d = w_in.shape[0]
    c = np.cumsum([0, W_M, W_M, W_M, W_M, H_M, H_M, W_A, W_A, W_A, H_I * D_I, D_I, H_I])
    seg = lambda i: w_in[:, c[i]:c[i + 1]]
    wm = w_in[:, c[0]:c[4]]
    wa = jnp.concatenate([seg(6), seg(8)], axis=1) if transposed else w_in[:, c[6]:c[9]]
    wi = seg(9)
    ws = jnp.concatenate([seg(10), seg(4), seg(5), seg(11),
                          jnp.zeros((d, LANES - D_I - 2 * H_M - H_I), w_in.dtype)], axis=1)
    wt = jnp.concatenate([seg(7), seg(8), seg(10)], axis=1).T
    ws_hi = ws.astype(BF16)
    ws2 = jnp.stack([ws_hi, (ws - ws_hi.astype(F32)).astype(BF16)])
    return (wm.astype(BF16), wa.astype(BF16), wi.astype(BF16), ws2, wt.astype(BF16))


def _project(x2, w_in, nb, seq, tm, transposed):
    n, d = x2.shape
    packs = _pack_w_in(w_in, transposed)
    nps = seq // tm
    sds = jax.ShapeDtypeStruct
    row = lambda w, dt: (sds((n, w), dt), pl.BlockSpec((tm, w), lambda i: (i, 0)))
    tr = lambda w, dt: (sds((nb, w, seq), dt), pl.BlockSpec((None, w, tm), lambda i: (i // nps, 0, i % nps)))
    outs = dict(
        mq=row(W_M, BF16), mk=row(W_M, BF16), mv=row(W_M, BF16), mo=row(W_M, F32), sm=row(LANES, F32),
        aq=row(W_A, BF16), qs=row(H_I * D_I, BF16), lo=row(LANES, F32), hi=row(LANES, F32),
        avb=row(W_A, BF16), ktb=tr(W_A, BF16), iktb=tr(D_I, BF16), ktf=tr(W_A, F32), vtf=tr(W_A, F32),
        iktf=tr(D_I, F32), ak=row(W_A, F32), av=row(W_A, F32), ik=row(D_I, F32),
    )
    names = _PROJ_COMMON + _PROJ_EXTRA[transposed]
    res = pl.pallas_call(
        functools.partial(_proj_kernel, transposed=transposed),
        grid=(n // tm,),
        in_specs=[pl.BlockSpec((tm, d), lambda i: (i, 0))] + [_const_spec(w.shape) for w in packs],
        out_specs=[outs[k][1] for k in names],
        out_shape=[outs[k][0] for k in names],
        compiler_params=_cparams(("arbitrary",)),
        name="proj_t" if transposed else "proj",
    )(x2, *packs)
    return dict(zip(names, res))


def _gates(sm, bias):
    cap = GATE_CAP * jnp.tanh((sm + bias) / GATE_CAP)
    e = jnp.exp(-jnp.abs(cap))
    u = 1.0 + e
    log1p_e = jnp.where(u == 1.0, e, jnp.log(u) * (e / (u - 1.0)))
    logsig = jnp.minimum(cap, 0.0) - log1p_e
    lane = lax.broadcasted_iota(I32, sm.shape, 1)
    return jnp.where(lane >= SMALL_MF, logsig, cap)


def _split3(x):
    a = x.astype(BF16)
    r = x - a.astype(F32)
    b = r.astype(BF16)
    c = (r - b.astype(F32)).astype(BF16)
    return a, b, c


def _dot3(mat, parts):
    return sum(jnp.dot(mat, p, preferred_element_type=F32) for p in parts)


def _headnorm_out(hh, mo, g):
    mu = jnp.mean(hh, axis=1, keepdims=True)
    var = jnp.mean(jnp.square(hh - mu), axis=1, keepdims=True)
    hn = (hh - mu) * lax.rsqrt(var + LN_EPS) * g
    return (hn / (1.0 + jnp.exp(-mo))).astype(BF16)


def _mlstm_prompt_kernel(q_ref, k_ref, v_ref, mo_ref, sm_ref, bias_ref, g_ref,
                         out_ref, c_out, n_out, m_out, c_s, n_s, m_s, *, chunk):
    j = pl.program_id(1)

    @pl.when(j == 0)
    def _():
        c_s[...] = jnp.zeros_like(c_s)
        n_s[...] = jnp.zeros_like(n_s)
        m_s[...] = jnp.zeros_like(m_s)

    L = chunk
    gate = _gates(sm_ref[...], bias_ref[...])
    r_i = lax.broadcasted_iota(I32, (L, L), 0)
    c_i = lax.broadcasted_iota(I32, (L, L), 1)
    tril = c_i <= r_i
    cum = _dot3(jnp.where(tril, 1.0, 0.0).astype(BF16), _split3(gate))
    gate_t = gate.T
    cum_t = cum.T
    for h in range(H_M):
        sl = slice(h * D_HM, (h + 1) * D_HM)
        b_col = cum[:, SMALL_MF + h:SMALL_MF + h + 1]
        b_row = cum_t[SMALL_MF + h:SMALL_MF + h + 1, :]
        i_col = gate[:, SMALL_MI + h:SMALL_MI + h + 1]
        i_row = gate_t[SMALL_MI + h:SMALL_MI + h + 1, :]
        m_prev = m_s[h][:, 0:1]
        dmat = jnp.where(tril, b_col - b_row + i_row, -jnp.inf)
        inter = b_col + m_prev
        mt = jnp.maximum(inter, jnp.max(dmat, axis=1, keepdims=True))
        w_intra = jnp.exp(dmat - mt)
        w_inter = jnp.exp(inter - mt)
        q, k, v = q_ref[:, sl], k_ref[:, sl], v_ref[:, sl]
        s = lax.dot_general(q, k, _NT, preferred_element_type=F32) * w_intra
        c_old = c_s[h]
        n_old = n_s[h]
        num = (jnp.dot(s.astype(BF16), v, preferred_element_type=F32)
               + w_inter * lax.dot_general(q, c_old.astype(BF16), _NT, preferred_element_type=F32))
        den = (jnp.sum(s, axis=1, keepdims=True)
               + w_inter * jnp.sum(q.astype(F32) * n_old, axis=1, keepdims=True))
        hh = num / jnp.maximum(jnp.abs(den), jnp.exp(-mt))
        out_ref[:, sl] = _headnorm_out(hh, mo_ref[:, sl], g_ref[:, sl])
        bl = b_row[:, L - 1:L]
        decay = bl - b_col + i_col
        m_new = jnp.maximum(bl + m_prev, jnp.max(decay, axis=0, keepdims=True))
        ws = jnp.exp(decay - m_new)
        cs = jnp.exp(bl + m_prev - m_new)
        vw_t = (v.astype(F32) * ws).T.astype(BF16)
        c_s[h] = cs * c_old + jnp.dot(vw_t, k, preferred_element_type=F32)
        n_s[h] = cs * n_old + jnp.sum(k.astype(F32) * ws, axis=0, keepdims=True)
        m_s[h] = jnp.broadcast_to(m_new, (1, LANES))

    @pl.when(j == pl.num_programs(1) - 1)
    def _():
        c_out[...] = c_s[...]
        n_out[...] = n_s[...]
        m_out[...] = m_s[...]


def _gate_bias(b_ig, b_fg):
    z = jnp.zeros((LANES,), F32)
    z = z.at[SMALL_MI:SMALL_MI + H_M].set(b_ig.astype(F32)).at[SMALL_MF:SMALL_MF + H_M].set(b_fg.astype(F32))
    return z.reshape(1, LANES)


def _mlstm_prompt(pr, bias, norm_g, nb, seq, chunk):
    n = nb * seq
    nc = seq // chunk
    row = lambda w: pl.BlockSpec((chunk, w), lambda b, j: (b * nc + j, 0))
    st = lambda *s: pl.BlockSpec((None,) + s, lambda b, j: (b,) + (0,) * len(s))
    sds = jax.ShapeDtypeStruct
    out, c, nn, m = pl.pallas_call(
        functools.partial(_mlstm_prompt_kernel, chunk=chunk),
        grid=(nb, nc),
        in_specs=[row(W_M), row(W_M), row(W_M), row(W_M), row(LANES),
                  pl.BlockSpec((1, LANES), lambda b, j: (0, 0)), pl.BlockSpec((1, W_M), lambda b, j: (0, 0))],
        out_specs=[row(W_M), st(H_M, D_HM, D_HM), st(H_M, 1, D_HM), st(H_M, 1, LANES)],
        out_shape=[sds((n, W_M), BF16), sds((nb, H_M, D_HM, D_HM), F32),
                   sds((nb, H_M, 1, D_HM), F32), sds((nb, H_M, 1, LANES), F32)],
        scratch_shapes=[pltpu.VMEM((H_M, D_HM, D_HM), F32), pltpu.VMEM((H_M, 1, D_HM), F32),
                        pltpu.VMEM((H_M, 1, LANES), F32)],
        compiler_params=_cparams(("arbitrary", "arbitrary")),
        name="mlstm_prompt",
    )(pr["mq"], pr["mk"], pr["mv"], pr["mo"], pr["sm"], bias, norm_g.reshape(1, W_M).astype(F32))
    return out, c, nn.reshape(nb, H_M, D_HM), m[:, :, 0, 0]


def _mlstm_sample_kernel(q_ref, k_ref, v_ref, mo_ref, sm_ref, bias_ref, g_ref, mrow_ref, nrow_ref,
                         c_in, n_in, out_ref, c_out, n_out, m_out, numi_s, *, steps):
    h = pl.program_id(1)
    R = q_ref.shape[0]
    nseq = R // steps
    gate = _gates(sm_ref[...], bias_ref[...])
    r_i = lax.broadcasted_iota(I32, (R, R), 0)
    c_i = lax.broadcasted_iota(I32, (R, R), 1)
    same = (r_i // steps) == (c_i // steps)
    mask = jnp.logical_and(same, c_i <= r_i)
    parts = _split3(gate)
    cum = _dot3(jnp.where(mask, 1.0, 0.0).astype(BF16), parts)
    tot = _dot3(jnp.where(same, 1.0, 0.0).astype(BF16), parts)
    lane = lax.broadcasted_iota(I32, (R, LANES), 1)
    sub = lax.broadcasted_iota(I32, (LANES, R), 0)
    col = lambda a, off: jnp.sum(jnp.where(lane == off + h, a, 0.0), axis=1, keepdims=True)
    rowv = lambda a, off: jnp.sum(jnp.where(sub == off + h, a.T, 0.0), axis=0, keepdims=True)
    b_col, b_row = col(cum, SMALL_MF), rowv(cum, SMALL_MF)
    i_col, i_row = col(gate, SMALL_MI), rowv(gate, SMALL_MI)
    bl_col, bl_row = col(tot, SMALL_MF), rowv(tot, SMALL_MF)
    m_col = jnp.sum(jnp.where(lane == h, mrow_ref[...], 0.0), axis=1, keepdims=True)
    dmat = jnp.where(mask, b_col - b_row + i_row, -jnp.inf)
    inter = b_col + m_col
    mt = jnp.maximum(inter, jnp.max(dmat, axis=1, keepdims=True))
    w_intra = jnp.exp(dmat - mt)
    w_inter = jnp.exp(inter - mt)
    q, k, v = q_ref[...], k_ref[...], v_ref[...]
    s = lax.dot_general(q, k, _NT, preferred_element_type=F32) * w_intra
    for b in range(nseq):
        rows = slice(b * steps, (b + 1) * steps)
        numi_s[rows, :] = lax.dot_general(q[rows, :], c_in[b].astype(BF16), _NT, preferred_element_type=F32)
    num = jnp.dot(s.astype(BF16), v, preferred_element_type=F32) + w_inter * numi_s[...]
    den = (jnp.sum(s, axis=1, keepdims=True)
           + w_inter * jnp.sum(q.astype(F32) * nrow_ref[...], axis=1, keepdims=True))
    hh = num / jnp.maximum(jnp.abs(den), jnp.exp(-mt))
    out_ref[...] = _headnorm_out(hh, mo_ref[...], g_ref[...])
    decay_col = bl_col - b_col + i_col
    decay_row = bl_row - b_row + i_row
    seg_max = jnp.max(jnp.where(same, decay_row, -jnp.inf), axis=1, keepdims=True)
    m_new = jnp.maximum(bl_col + m_col, seg_max)
    ws = jnp.exp(decay_col - m_new)
    cs = jnp.exp(bl_col + m_col - m_new)
    kf = k.astype(F32)
    vw_t = (v.astype(F32) * ws).T.astype(BF16)
    kw = kf * ws
    rid = lax.broadcasted_iota(I32, (R, 1), 0) // steps
    for b in range(nseq):
        r0 = b * steps
        kb = jnp.where(rid == b, kf, 0.0).astype(BF16)
        cs_b = cs[r0:r0 + 1, :]
        c_out[b] = cs_b * c_in[b] + jnp.dot(vw_t, kb, preferred_element_type=F32)
        n_out[b] = cs_b * n_in[b] + jnp.sum(kw[r0:r0 + steps, :], axis=0, keepdims=True)
        m_out[b] = jnp.broadcast_to(m_new[r0:r0 + 1, :], (1, LANES))


def _mlstm_sample(pr, bias, norm_g, state_c, state_n, state_m, nseq_total, steps):
    n = nseq_total * steps
    R = LANES
    spt = R // steps
    m_rows = jnp.pad(jnp.repeat(state_m.astype(F32), steps, axis=0), ((0, 0), (0, LANES - H_M)))
    n_rows = jnp.repeat(state_n.astype(F32).reshape(nseq_total, W_M), steps, axis=0)
    n4 = state_n.astype(F32).reshape(nseq_total, H_M, 1, D_HM)
    colh = lambda w: pl.BlockSpec((R, w), lambda i, h: (i, h))
    sth = lambda *s: pl.BlockSpec((spt, None) + s, lambda i, h: (i, h) + (0,) * len(s))
    sds = jax.ShapeDtypeStruct
    out, c, nn, m = pl.pallas_call(
        functools.partial(_mlstm_sample_kernel, steps=steps),
        grid=(n // R, H_M),
        in_specs=[colh(D_HM), colh(D_HM), colh(D_HM), colh(D_HM),
                  pl.BlockSpec((R, LANES), lambda i, h: (i, 0)),
                  pl.BlockSpec((1, LANES), lambda i, h: (0, 0)),
                  pl.BlockSpec((1, D_HM), lambda i, h: (0, h)),
                  pl.BlockSpec((R, LANES), lambda i, h: (i, 0)),
                  colh(D_HM), sth(D_HM, D_HM), sth(1, D_HM)],
        out_specs=[colh(D_HM), sth(D_HM, D_HM), sth(1, D_HM), sth(1, LANES)],
        out_shape=[sds((n, W_M), BF16), sds((nseq_total, H_M, D_HM, D_HM), F32),
                   sds((nseq_total, H_M, 1, D_HM), F32), sds((nseq_total, H_M, 1, LANES), F32)],
        scratch_shapes=[pltpu.VMEM((R, D_HM), F32)],
        compiler_params=_cparams(("arbitrary", "arbitrary")),
        name="mlstm_sample",
    )(pr["mq"], pr["mk"], pr["mv"], pr["mo"], pr["sm"], bias, norm_g.reshape(1, W_M).astype(F32),
      m_rows, n_rows, state_c.astype(F32), n4)
    return out, c, nn.reshape(nseq_total, H_M, D_HM), m[:, :, 0, 0]


def _rel_bucket(dist):
    max_exact = N_BUCKETS // 2
    d = jnp.maximum(dist, 0)
    df = jnp.maximum(d.astype(F32), 1.0)
    large = max_exact + (jnp.log(df / max_exact) / math.log(MAX_DISTANCE / max_exact)
                         * (N_BUCKETS - max_exact)).astype(I32)
    large = jnp.minimum(large, N_BUCKETS - 1)
    return jnp.where(d < max_exact, d, large)


def _bias_delta(rel_bias, dist):
    far = rel_bias[_rel_bucket(jnp.asarray(4 * MAX_DISTANCE, I32))]
    return (rel_bias[_rel_bucket(dist)] - far).astype(F32)


def _sortable(score):
    score = jnp.where(score == 0.0, 0.0, score)
    bits = pltpu.bitcast(score, I32)
    return bits ^ ((bits >> 31) & 0x7FFFFFFF)


def _kth_largest_search(count_ge, total, short, topk, rows):
    def cond(st):
        bit, _, _, done = st
        return jnp.logical_and(bit >= 0, jnp.min(done) < 0.5)

    def body(st):
        bit, pref, cnt, done = st
        cand_u = pref | lax.shift_left(jnp.int32(1), bit)
        c = count_ge(cand_u ^ INT_MIN)
        take = jnp.logical_and(c >= topk, done < 0.5)
        pref = jnp.where(take, cand_u, pref)
        cnt = jnp.where(take, c, cnt)
        done = jnp.where(cnt == topk, 1.0, done)
        return bit - 1, pref, cnt, done

    init = (jnp.int32(31), jnp.zeros((rows, 1), I32), jnp.full((rows, 1), total, F32),
            jnp.where(short, 1.0, 0.0))
    _, pref, _, done = lax.while_loop(cond, body, init)
    thr = jnp.where(short, INT_MIN + 1, pref ^ INT_MIN)
    return thr, done < 0.5


def _dsa_prompt_kernel(qs_ref, lo_ref, hi_ref, aq_ref, ikt_ref, kt_ref, v_ref, dz_ref, o_ref,
                       buf_ref, qm_ref, m_ref, l_ref, acc_ref, *, tq, topk):
    CH = KEY_CHUNK
    q0 = pl.program_id(1) * tq
    nchunk = jnp.maximum(lax.shift_right_logical(q0 + tq + CH - 1, 9), TAIL // CH + 1)

    qs = qs_ref[...]
    lo = lo_ref[...]
    hi = hi_ref[...]
    qpos = q0 + lax.broadcasted_iota(I32, (tq, CH), 0)
    lane_ch = lax.broadcasted_iota(I32, (tq, CH), 1)

    def score_chunk(c, carry):
        st = pl.multiple_of(c * CH, CH)
        ik = ikt_ref[:, pl.ds(st, CH)]
        sc = jnp.zeros((tq, CH), F32)
        for h in range(H_I):
            a = jnp.dot(qs[:, h * D_I:(h + 1) * D_I], ik, preferred_element_type=F32)
            sc = sc + jnp.maximum(jnp.minimum(a, hi[:, SMALL_IW + h:SMALL_IW + h + 1]),
                                  lo[:, SMALL_IW + h:SMALL_IW + h + 1])
        key = jnp.where(st + lane_ch <= qpos, _sortable(sc), INT_MIN)
        buf_ref[:, pl.ds(st, CH)] = key
        return carry

    lax.fori_loop(0, nchunk, score_chunk, 0)

    def count_ge(cand):
        candb = jnp.broadcast_to(cand, (tq, LANES))

        def body(c, acc):
            st = c * CH
            for g in range(CH // LANES):
                blk = buf_ref[:, pl.ds(pl.multiple_of(st + g * LANES, LANES), LANES)]
                acc = acc + jnp.where(blk >= candb, 1.0, 0.0)
            return acc

        acc = lax.fori_loop(0, nchunk, body, jnp.zeros((tq, LANES), F32))
        return jnp.sum(acc, axis=1, keepdims=True)

    nvalid = q0 + 1 + lax.broadcasted_iota(I32, (tq, 1), 0)
    short = nvalid <= topk
    thr, tie = _kth_largest_search(count_ge, (nchunk * CH).astype(F32), short, topk, tq)
    any_tie = jnp.max(jnp.where(tie, 1.0, 0.0)) > 0.5

    @pl.when(jnp.logical_not(any_tie))
    def _():
        def fin(c, carry):
            st = pl.multiple_of(c * CH, CH)
            blk = buf_ref[:, pl.ds(st, CH)]
            buf_ref[:, pl.ds(st, CH)] = jnp.where(blk >= thr, 0, NEG_BITS)
            return carry
        lax.fori_loop(0, nchunk, fin, 0)

    @pl.when(any_tie)
    def _():
        need = topk - count_ge(thr + 1)
        upper = (lax.broadcasted_iota(I32, (CH, CH), 0) < lax.broadcasted_iota(I32, (CH, CH), 1))
        upper = jnp.where(upper, 1.0, 0.0).astype(BF16)

        def fin(c, seen):
            st = pl.multiple_of(c * CH, CH)
            blk = buf_ref[:, pl.ds(st, CH)]
            eq = jnp.where(blk == thr, 1.0, 0.0)
            before = jnp.dot(eq.astype(BF16), upper, preferred_element_type=F32) + seen
            sel = jnp.logical_or(blk > thr, jnp.logical_and(blk == thr, before < need))
            buf_ref[:, pl.ds(st, CH)] = jnp.where(sel, 0, NEG_BITS)
            return seen + jnp.sum(eq, axis=1, keepdims=True)
        lax.fori_loop(0, nchunk, fin, jnp.zeros((tq, 1), F32))

    lane = lax.broadcasted_iota(I32, (tq, LANES), 1)
    left = lane < D_HA
    for hp in range(H_A // 2):
        aqp = aq_ref[:, hp * LANES:(hp + 1) * LANES]
        zero = jnp.zeros_like(aqp)
        qm_ref[2 * hp] = jnp.where(left, aqp, zero)
        qm_ref[2 * hp + 1] = jnp.where(left, zero, aqp)
    m_ref[...] = jnp.full(m_ref.shape, NEG, F32)
    l_ref[...] = jnp.zeros(l_ref.shape, F32)
    acc_ref[...] = jnp.zeros(acc_ref.shape, F32)

    def attend(st, width, extra, dz_start):
        mb = pltpu.bitcast(buf_ref[:, pl.ds(st, width)], F32)
        if extra is not None:
            mb = mb + extra
        for hp in range(H_A // 2):
            cols = slice(hp * LANES, (hp + 1) * LANES)
            kt = kt_ref[cols, pl.ds(st, width)]
            vv = v_ref[pl.ds(st, width), cols]
            accp = acc_ref[:, cols]
            new = []
            for e in range(2):
                h = 2 * hp + e
                s = jnp.dot(qm_ref[h], kt, preferred_element_type=F32) + mb
                if dz_start is not None:
                    s = s + dz_ref[h, :, pl.ds(dz_start, width)]
                m_old = m_ref[h]
                m_new = jnp.maximum(m_old, jnp.max(s, axis=1, keepdims=True))
                alpha = jnp.exp(m_old - m_new)
                p = jnp.exp(s - m_new)
                l_ref[h] = alpha * l_ref[h] + jnp.sum(p, axis=1, keepdims=True)
                m_ref[h] = m_new
                new.append(alpha * accp + jnp.dot(p.astype(BF16), vv, preferred_element_type=F32))
            acc_ref[:, cols] = jnp.where(left, new[0], new[1])

    ts = jnp.maximum(q0 - CH, 0)
    nfar = lax.shift_right_logical(ts + CH - 1, 9)

    def far(jc, carry):
        attend(pl.multiple_of(jc * CH, CH), CH, None, None)
        return carry

    lax.fori_loop(0, nfar, far, 0)
    lim = nfar * CH - ts
    covered = jnp.where(lax.broadcasted_iota(I32, (tq, TAIL), 1) < lim, NEG, 0.0)
    attend(pl.multiple_of(ts, LANES), TAIL, covered, pl.multiple_of(CH - (q0 - ts), LANES))

    for hp in range(H_A // 2):
        cols = slice(hp * LANES, (hp + 1) * LANES)
        inv = jnp.where(left, 1.0 / l_ref[2 * hp], 1.0 / l_ref[2 * hp + 1])
        o_ref[:, cols] = acc_ref[:, cols] * inv


def _dsa_prompt(pr, rel_bias, nb, seq, tq):
    topk = min(TOPK_MAX, seq // 4)
    assert seq % KEY_CHUNK == 0 and seq >= TAIL + KEY_CHUNK and tq == LANES
    wtab = KEY_CHUNK + TAIL
    dist = (jnp.arange(tq, dtype=I32)[:, None] - jnp.arange(wtab, dtype=I32)[None, :] + KEY_CHUNK)
    dz = jnp.transpose(_bias_delta(rel_bias, dist), (2, 0, 1))
    nq = seq // tq
    rowq = lambda w: pl.BlockSpec((tq, w), lambda b, i: (b * nq + i, 0))
    perb = lambda r, c: pl.BlockSpec((None, r, c), lambda b, i: (b, 0, 0), pipeline_mode=pl.Buffered(1))
    vb = pr["avb"].reshape(nb, seq, W_A)
    return pl.pallas_call(
        functools.partial(_dsa_prompt_kernel, tq=tq, topk=topk),
        grid=(nb, nq),
        in_specs=[rowq(H_I * D_I), rowq(LANES), rowq(LANES), rowq(W_A),
                  perb(D_I, seq), perb(W_A, seq), perb(seq, W_A), _const_spec((H_A, tq, wtab))],
        out_specs=rowq(W_A),
        out_shape=jax.ShapeDtypeStruct((nb * seq, W_A), F32),
        scratch_shapes=[pltpu.VMEM((tq, seq), I32), pltpu.VMEM((H_A, tq, LANES), BF16),
                        pltpu.VMEM((H_A, tq, 1), F32), pltpu.VMEM((H_A, tq, 1), F32),
                        pltpu.VMEM((tq, W_A), F32)],
        compiler_params=_cparams(("arbitrary", "arbitrary")),
        name="dsa_prompt",
    )(pr["qs"], pr["lo"], pr["hi"], pr["aq"], pr["iktb"], pr["ktb"], vb, dz)


def _dsa_sample_kernel(pt_ref, qs_ref, lo_ref, hi_ref, aq_ref, ikn_ref, akn_ref, avn_ref, dz_ref,
                       cki_ref, ck_ref, cv_ref, o_ref,
                       kib, kbuf, vbuf, buf_ref, sem_ki, sem_kv, *, steps, topk, n_pages, page, cpp):
    b = pl.program_id(0)
    nb = pl.num_programs(0)
    past = n_pages * page
    nch = n_pages // cpp
    ckeys = cpp * page
    width = past + LANES
    rows_a = H_A * steps

    def ki_copy(seq_i, slot, p):
        return pltpu.make_async_copy(cki_ref.at[pt_ref[seq_i, p]], kib.at[slot, p], sem_ki.at[slot])

    def kv_copies(seq_i, c, slot, p):
        pg = pt_ref[seq_i, c * cpp + p]
        return (pltpu.make_async_copy(ck_ref.at[pg], kbuf.at[slot, p], sem_kv.at[slot]),
                pltpu.make_async_copy(cv_ref.at[pg], vbuf.at[slot, p], sem_kv.at[slot]))

    def start_ki(seq_i, slot):
        def f(p, carry):
            ki_copy(seq_i, slot, p).start()
            return carry
        lax.fori_loop(0, n_pages, f, 0)

    def start_kv(seq_i, c, slot):
        def f(p, carry):
            for cp in kv_copies(seq_i, c, slot, p):
                cp.start()
            return carry
        lax.fori_loop(0, cpp, f, 0)

    def wait_kv(seq_i, c, slot):
        def f(p, carry):
            for cp in kv_copies(seq_i, c, slot, p):
                cp.wait()
            return carry
        lax.fori_loop(0, cpp, f, 0)

    slot_b = b % 2

    @pl.when(b == 0)
    def _():
        start_ki(0, 0)
        start_kv(0, 0, 0)

    def wait_ki(p, carry):
        ki_copy(b, slot_b, p).wait()
        return carry
    lax.fori_loop(0, n_pages, wait_ki, 0)

    @pl.when(b + 1 < nb)
    def _():
        start_ki(b + 1, 1 - slot_b)

    qs = qs_ref[...]
    lo = lo_ref[...]
    hi = hi_ref[...]
    qs_hm = jnp.concatenate([qs[:, h * D_I:(h + 1) * D_I] for h in range(H_I)], axis=0)
    lo_hm = jnp.concatenate([lo[:, SMALL_IW + h:SMALL_IW + h + 1] for h in range(H_I)], axis=0)
    hi_hm = jnp.concatenate([hi[:, SMALL_IW + h:SMALL_IW + h + 1] for h in range(H_I)], axis=0)

    def head_sum(a):
        t = jnp.maximum(jnp.minimum(a, hi_hm), lo_hm)
        return sum(t[h * steps:(h + 1) * steps] for h in range(H_I))

    def score_page(p, carry):
        a = jnp.dot(qs_hm, kib[slot_b, p].astype(BF16), preferred_element_type=F32)
        buf_ref[:, pl.ds(pl.multiple_of(p * page, page), page)] = _sortable(head_sum(a))
        return carry
    lax.fori_loop(0, n_pages, score_page, 0, unroll=4)

    ikn = jnp.concatenate([ikn_ref[...], jnp.zeros((LANES - steps, D_I), F32)], axis=0).astype(BF16)
    a_new = lax.dot_general(qs_hm, ikn, _NT, preferred_element_type=F32)
    t_i = lax.broadcasted_iota(I32, (steps, LANES), 0)
    j_i = lax.broadcasted_iota(I32, (steps, LANES), 1)
    buf_ref[:, past:width] = jnp.where(j_i <= t_i, _sortable(head_sum(a_new)), INT_MIN)

    def count_ge(cand):
        return jnp.sum(jnp.where(buf_ref[...] >= cand, 1.0, 0.0), axis=1, keepdims=True)

    nvalid = past + 1 + lax.broadcasted_iota(I32, (steps, 1), 0)
    short = nvalid <= topk
    thr, tie = _kth_largest_search(count_ge, jnp.float32(width), short, topk, steps)
    any_tie = jnp.max(jnp.where(tie, 1.0, 0.0)) > 0.5

    @pl.when(jnp.logical_not(any_tie))
    def _():
        buf_ref[...] = jnp.where(buf_ref[...] >= thr, 0, NEG_BITS)

    @pl.when(any_tie)
    def _():
        need = topk - count_ge(thr + 1)
        TW = LANES
        upper = (lax.broadcasted_iota(I32, (TW, TW), 0) < lax.broadcasted_iota(I32, (TW, TW), 1))
        upper = jnp.where(upper, 1.0, 0.0).astype(BF16)

        def fin(c, seen):
            st = pl.multiple_of(c * TW, TW)
            blk = buf_ref[:, pl.ds(st, TW)]
            eq = jnp.where(blk == thr, 1.0, 0.0)
            before = jnp.dot(eq.astype(BF16), upper, preferred_element_type=F32) + seen
            sel = jnp.logical_or(blk > thr, jnp.logical_and(blk == thr, before < need))
            buf_ref[:, pl.ds(st, TW)] = jnp.where(sel, 0, NEG_BITS)
            return seen + jnp.sum(eq, axis=1, keepdims=True)
        lax.fori_loop(0, width // TW, fin, jnp.zeros((steps, 1), F32))

    r_head = lax.broadcasted_iota(I32, (rows_a, W_A), 0) // steps
    c_head = lax.broadcasted_iota(I32, (rows_a, W_A), 1) // D_HA
    own = r_head == c_head
    aq_t = jnp.concatenate([aq_ref[...]] * H_A, axis=0)
    q_bd = jnp.where(own, aq_t, jnp.zeros_like(aq_t))
    tile_rows = lambda a: jnp.concatenate([a] * H_A, axis=0)

    def softmax_step(state, s, pv_fn):
        m_old, l_old, acc = state
        m_new = jnp.maximum(m_old, jnp.max(s, axis=1, keepdims=True))
        alpha = jnp.exp(m_old - m_new)
        p = jnp.exp(s - m_new)
        return m_new, alpha * l_old + jnp.sum(p, axis=1, keepdims=True), alpha * acc + pv_fn(p.astype(BF16))

    def chunk_step(c, state, last):
        g = b * nch + c
        slot = g % 2
        wait_kv(b, c, slot)

        @pl.when(g + 1 < nb * nch)
        def _():
            nxt = g + 1
            start_kv(nxt // nch, nxt % nch, 1 - slot)

        s = jnp.concatenate([jnp.dot(q_bd, kbuf[slot, p].astype(BF16), preferred_element_type=F32)
                             for p in range(cpp)], axis=1)
        s = s + tile_rows(pltpu.bitcast(buf_ref[:, pl.ds(pl.multiple_of(c * ckeys, ckeys), ckeys)], F32))
        if last:
            s = s + jnp.concatenate([jnp.zeros((rows_a, ckeys - page), F32), dz_ref[:, 0:page]], axis=1)

        def pv(pb):
            return sum(lax.dot_general(pb[:, p * page:(p + 1) * page], vbuf[slot, p].astype(BF16), _NT,
                                       preferred_element_type=F32) for p in range(cpp))
        return softmax_step(state, s, pv)

    state = (jnp.full((rows_a, 1), NEG, F32), jnp.zeros((rows_a, 1), F32), jnp.zeros((rows_a, W_A), F32))
    state = lax.fori_loop(0, nch - 1, lambda c, st: chunk_step(c, st, False), state)
    state = chunk_step(nch - 1, state, True)
    pad = jnp.zeros((LANES - steps, W_A), F32)
    kn = jnp.concatenate([akn_ref[...], pad], axis=0).astype(BF16)
    vn = jnp.concatenate([avn_ref[...], pad], axis=0).astype(BF16)
    s_new = (lax.dot_general(q_bd, kn, _NT, preferred_element_type=F32)
             + tile_rows(pltpu.bitcast(buf_ref[:, past:width], F32)) + dz_ref[:, page:2 * page])
    _, l_fin, acc = softmax_step(state, s_new, lambda pb: jnp.dot(pb, vn, preferred_element_type=F32))
    accn = jnp.where(own, acc / l_fin, 0.0)
    o_ref[...] = sum(accn[h * steps:(h + 1) * steps] for h in range(H_A))


def _dsa_sample(pr, cache_k, cache_v, cache_k_idx, page_table, rel_bias, nseq, steps):
    n_pool, page = cache_k.shape[0], cache_k.shape[1]
    n_pages = page_table.shape[1]
    past = n_pages * page
    topk = min(TOPK_MAX, (past + steps) // 4)
    cpp = 8 if n_pages % 8 == 0 else n_pages
    assert page == LANES and steps <= LANES
    t = jnp.arange(steps, dtype=I32)[:, None]
    c = jnp.arange(LANES, dtype=I32)[None, :]
    dz = jnp.transpose(_bias_delta(rel_bias, jnp.concatenate([t + LANES - c, t - c], axis=1)), (2, 0, 1))
    dz = dz.reshape(H_A * steps, 2 * LANES)
    ck_t = jnp.transpose(cache_k, (0, 2, 3, 1)).reshape(n_pool, W_A, page)
    cv_t = jnp.transpose(cache_v, (0, 2, 3, 1)).reshape(n_pool, W_A, page)
    cki_t = jnp.transpose(cache_k_idx, (0, 2, 1))
    rows = lambda w: pl.BlockSpec((steps, w), lambda b, pt: (b, 0))
    anyspec = pl.BlockSpec(memory_space=pl.ANY)
    return pl.pallas_call(
        functools.partial(_dsa_sample_kernel, steps=steps, topk=topk, n_pages=n_pages, page=page, cpp=cpp),
        grid_spec=pltpu.PrefetchScalarGridSpec(
            num_scalar_prefetch=1,
            grid=(nseq,),
            in_specs=[rows(H_I * D_I), rows(LANES), rows(LANES), rows(W_A), rows(D_I), rows(W_A), rows(W_A),
                      pl.BlockSpec((H_A * steps, 2 * LANES), lambda b, pt: (0, 0)),
                      anyspec, anyspec, anyspec],
            out_specs=rows(W_A),
            scratch_shapes=[pltpu.VMEM((2, n_pages, D_I, page), F32),
                            pltpu.VMEM((2, cpp, W_A, page), F32), pltpu.VMEM((2, cpp, W_A, page), F32),
                            pltpu.VMEM((steps, past + LANES), I32),
                            pltpu.SemaphoreType.DMA((2,)), pltpu.SemaphoreType.DMA((2,))],
        ),
        out_shape=jax.ShapeDtypeStruct((nseq * steps, W_A), F32),
        compiler_params=_cparams(("arbitrary",)),
        name="dsa_sample",
    )(page_table.astype(I32), pr["qs"], pr["lo"], pr["hi"], pr["aq"], pr["ik"], pr["ak"], pr["av"], dz,
      cki_t, ck_t, cv_t)


def _layer_norm(y, g, b):
    mu = jnp.mean(y, axis=1, keepdims=True)
    var = jnp.mean(jnp.square(y - mu), axis=1, keepdims=True)
    return (y - mu) * lax.rsqrt(var + LN_EPS) * g + b


def _mix_router_kernel(x_ref, mo_ref, ao_ref, wo1_ref, wo2_ref, g_ref, b_ref, rw_ref, rb_ref,
                       h_ref, route_ref, gate_ref, cnt_ref, carry_s):
    @pl.when(pl.program_id(0) == 0)
    def _():
        carry_s[...] = jnp.zeros_like(carry_s)

    tm = x_ref.shape[0]
    mix = (jnp.dot(mo_ref[...], wo1_ref[...], preferred_element_type=F32)
           + jnp.dot(ao_ref[...].astype(BF16), wo2_ref[...], preferred_element_type=F32))
    h = _layer_norm(DN_ALPHA * x_ref[...] + mix, g_ref[...], b_ref[...])
    h_ref[...] = h
    logits = jnp.dot(h.astype(BF16), rw_ref[...], preferred_element_type=F32) + rb_ref[...]
    lane = lax.broadcasted_iota(I32, (tm, LANES), 1)
    lane_f = lane.astype(F32)
    l = jnp.where(lane < N_EXPERTS, logits, -jnp.inf)
    vals, hots, idxs = [], [], []
    member = jnp.zeros((tm, LANES), F32)
    for _ in range(TOP_K_EXP):
        mx = jnp.max(l, axis=1, keepdims=True)
        idx = jnp.min(jnp.where(l == mx, lane_f, float(LANES)), axis=1, keepdims=True)
        hot = lane_f == idx
        vals.append(mx)
        idxs.append(idx)
        hots.append(hot)
        member = member + jnp.where(hot, 1.0, 0.0)
        l = jnp.where(hot, -jnp.inf, l)
    ex = [jnp.exp(v - vals[0]) for v in vals]
    tot = sum(ex)
    stril = (lax.broadcasted_iota(I32, (tm, tm), 1) < lax.broadcasted_iota(I32, (tm, tm), 0))
    before = jnp.dot(jnp.where(stril, 1.0, 0.0).astype(BF16), member.astype(BF16),
                     preferred_element_type=F32) + carry_s[...]
    route = jnp.zeros((tm, LANES), F32)
    gates = jnp.zeros((tm, LANES), F32)
    for k in range(TOP_K_EXP):
        rank = jnp.sum(jnp.where(hots[k], before, 0.0), axis=1, keepdims=True)
        route = route + jnp.where(lane == k, idxs[k], 0.0) + jnp.where(lane == TOP_K_EXP + k, rank, 0.0)
        gates = gates + jnp.where(lane == k, ex[k] / tot, 0.0)
    route_ref[...] = route.astype(I32)
    gate_ref[...] = gates
    carry_s[...] = carry_s[...] + jnp.sum(member, axis=0, keepdims=True)
    cnt_ref[...] = carry_s[...]


def _scatter_rows_kernel(dest_ref, h_ref, xs_in, xs_out, sem):
    del xs_in
    ts = h_ref.shape[0]

    def start(t, carry):
        for k in range(TOP_K_EXP):
            d = dest_ref[t * TOP_K_EXP + k]
            pltpu.make_async_copy(h_ref.at[pl.ds(t, 1)], xs_out.at[pl.ds(d, 1)], sem).start()
        return carry
    lax.fori_loop(0, ts, start, 0)

    def wait(t, carry):
        for k in range(TOP_K_EXP):
            pltpu.make_async_copy(h_ref.at[pl.ds(0, 1)], xs_out.at[pl.ds(0, 1)], sem).wait()
        return carry
    lax.fori_loop(0, ts, wait, 0)


def _expert_kernel(be_ref, na_ref, xs_ref, wgu_ref, bgu_ref, wdn_ref, bdn_ref, ys_ref):
    del be_ref
    active = pl.program_id(0) < na_ref[0]

    @pl.when(jnp.logical_not(active))
    def _():
        ys_ref[...] = jnp.zeros_like(ys_ref)

    @pl.when(active)
    def _():
        f = wdn_ref.shape[0]
        gu = jnp.dot(xs_ref[...].astype(BF16), wgu_ref[...], preferred_element_type=F32) + bgu_ref[...]
        gt = jnp.minimum(gu[:, :f], SWIGLU_LIMIT)
        up = jnp.clip(gu[:, f:], -SWIGLU_LIMIT, SWIGLU_LIMIT)
        act = (up + 1.0) * gt / (1.0 + jnp.exp(-SWIGLU_ALPHA * gt))
        ys_ref[...] = jnp.dot(act.astype(BF16), wdn_ref[...], preferred_element_type=F32) + bdn_ref[...]


def _combine_kernel(dest_ref, gate_ref, h_ref, g_ref, b_ref, ys_ref, o_ref, ybuf, sem):
    tc = h_ref.shape[0]

    def copy(t, k, d):
        return pltpu.make_async_copy(ys_ref.at[pl.ds(d, 1)], ybuf.at[k, pl.ds(t, 1)], sem)

    def start(t, carry):
        for k in range(TOP_K_EXP):
            copy(t, k, dest_ref[t * TOP_K_EXP + k]).start()
        return carry
    lax.fori_loop(0, tc, start, 0)

    def wait(t, carry):
        for k in range(TOP_K_EXP):
            copy(0, k, 0).wait()
        return carry
    lax.fori_loop(0, tc, wait, 0)
    gates = gate_ref[...]
    y = sum(gates[:, k:k + 1] * ybuf[k] for k in range(TOP_K_EXP))
    o_ref[...] = _layer_norm(DN_ALPHA * h_ref[...] + y, g_ref[...], b_ref[...])


def _finish(x2, m_out, a_out, ffn, tm, tr):
    n, d = x2.shape
    wo1, wo2, ln1_g, ln1_b, rw, rb, wgu, bgu, wdn, bdn, ln2_g, ln2_b = ffn
    f = wdn.shape[1]
    vec = lambda a: a.reshape(1, -1).astype(F32)
    row = lambda t, w: pl.BlockSpec((t, w), lambda i, *_: (i, 0))
    sds = jax.ShapeDtypeStruct
    h, route, gates, cnt = pl.pallas_call(
        _mix_router_kernel,
        grid=(n // tm,),
        in_specs=[row(tm, d), row(tm, W_M), row(tm, W_A), _const_spec(wo1.shape), _const_spec(wo2.shape),
                  _const_spec((1, d)), _const_spec((1, d)), _const_spec(rw.shape), _const_spec((1, LANES))],
        out_specs=[row(tm, d), row(tm, LANES), row(tm, LANES), pl.BlockSpec((1, LANES), lambda i: (0, 0))],
        out_shape=[sds((n, d), F32), sds((n, LANES), I32), sds((n, LANES), F32), sds((1, LANES), F32)],
        scratch_shapes=[pltpu.VMEM((1, LANES), F32)],
        compiler_params=_cparams(("arbitrary",)),
        name="mix_router",
    )(x2, m_out, a_out, wo1, wo2, vec(ln1_g), vec(ln1_b), rw, rb)

    counts = cnt[0, :N_EXPERTS].astype(I32)
    padded = (counts + ROW_BLOCK - 1) // ROW_BLOCK * ROW_BLOCK
    pad_end = jnp.cumsum(padded)
    pad_start = pad_end - padded
    dest = (pad_start[route[:, :TOP_K_EXP]] + route[:, TOP_K_EXP:2 * TOP_K_EXP]).reshape(-1)
    n_blk = -(-n * TOP_K_EXP // ROW_BLOCK) + N_EXPERTS
    blk_e = jnp.minimum(jnp.searchsorted(pad_end, jnp.arange(n_blk, dtype=I32) * ROW_BLOCK, side="right"),
                        N_EXPERTS - 1).astype(I32)
    n_act = (pad_end[-1:] // ROW_BLOCK).astype(I32)

    smem_dest = pl.BlockSpec((tr * TOP_K_EXP,), lambda i: (i,), memory_space=pltpu.SMEM)
    anyspec = pl.BlockSpec(memory_space=pl.ANY)
    xs = pl.pallas_call(
        _scatter_rows_kernel,
        grid=(n // tr,),
        in_specs=[smem_dest, row(tr, d), anyspec],
        out_specs=anyspec,
        out_shape=sds((n_blk * ROW_BLOCK, d), F32),
        scratch_shapes=[pltpu.SemaphoreType.DMA(())],
        input_output_aliases={2: 0},
        compiler_params=_cparams(("arbitrary",)),
        name="moe_scatter",
    )(dest, h, jnp.zeros((n_blk * ROW_BLOCK, d), F32))

    blk = lambda i, be, na: jnp.minimum(i, na[0] - 1)
    ys = pl.pallas_call(
        _expert_kernel,
        grid_spec=pltpu.PrefetchScalarGridSpec(
            num_scalar_prefetch=2,
            grid=(n_blk,),
            in_specs=[pl.BlockSpec((ROW_BLOCK, d), lambda i, be, na: (blk(i, be, na), 0)),
                      pl.BlockSpec((None, d, 2 * f), lambda i, be, na: (be[i], 0, 0)),
                      pl.BlockSpec((None, 1, 2 * f), lambda i, be, na: (be[i], 0, 0)),
                      pl.BlockSpec((None, f, d), lambda i, be, na: (be[i], 0, 0)),
                      pl.BlockSpec((None, 1, d), lambda i, be, na: (be[i], 0, 0))],
            out_specs=pl.BlockSpec((ROW_BLOCK, d), lambda i, be, na: (i, 0)),
        ),
        out_shape=sds((n_blk * ROW_BLOCK, d), F32),
        compiler_params=_cparams(("arbitrary",)),
        name="moe_experts",
    )(blk_e, n_act, xs, wgu, bgu, wdn, bdn)

    return pl.pallas_call(
        _combine_kernel,
        grid=(n // tr,),
        in_specs=[smem_dest, row(tr, LANES), row(tr, d), _const_spec((1, d)), _const_spec((1, d)), anyspec],
        out_specs=row(tr, d),
        out_shape=sds((n, d), F32),
        scratch_shapes=[pltpu.VMEM((TOP_K_EXP, tr, d), F32), pltpu.SemaphoreType.DMA(())],
        compiler_params=_cparams(("arbitrary",)),
        name="moe_combine",
    )(dest, gates, h, vec(ln2_g), vec(ln2_b), ys)


def _pack_ffn(w_out, ln1_g, ln1_b, router_w, router_b, w_gu, b_gu, w_dn, b_dn, ln2_g, ln2_b):
    d = w_out.shape[1]
    rw = jnp.pad(router_w, ((0, 0), (0, LANES - N_EXPERTS))).astype(BF16)
    rb = jnp.pad(router_b.astype(F32), (0, LANES - N_EXPERTS)).reshape(1, LANES)
    return (w_out[:W_M].astype(BF16), w_out[W_M:].astype(BF16), ln1_g, ln1_b, rw, rb,
            w_gu.astype(BF16), b_gu.astype(F32).reshape(N_EXPERTS, 1, -1),
            w_dn.astype(BF16), b_dn.astype(F32).reshape(N_EXPERTS, 1, d), ln2_g, ln2_b)


def _tile(n, pref):
    t = pref
    while n % t:
        t //= 2
    return t


def kernel(x_prompt, x_sample, cache_k, cache_v, cache_k_idx, state_C, state_n, state_m, page_table,
           w_in, b_ig, b_fg, mlstm_norm_g, w_out, ln1_g, ln1_b, router_w, router_b,
           w_gate_up, b_gate_up, w_down, b_down, ln2_g, ln2_b, rel_bias):
    assert w_in.shape[0] == DEPTH
    nb, seq, d = x_prompt.shape
    ns, steps, _ = x_sample.shape
    bias = _gate_bias(b_ig[0], b_fg[0])
    ffn = _pack_ffn(w_out[0], ln1_g[0], ln1_b[0], router_w[0], router_b[0], w_gate_up[0], b_gate_up[0],
                    w_down[0], b_down[0], ln2_g[0], ln2_b[0])

    xp = x_prompt.reshape(nb * seq, d)
    pp = _project(xp, w_in[0], nb, seq, _tile(seq, 256), True)
    mo_p, c_p, n_p, m_p = _mlstm_prompt(pp, bias, mlstm_norm_g[0], nb, seq, LANES)
    ao_p = _dsa_prompt(pp, rel_bias, nb, seq, LANES)
    y_p = _finish(xp, mo_p, ao_p, ffn, _tile(nb * seq, 512), _tile(nb * seq, 256))

    xs = x_sample.reshape(ns * steps, d)
    ps = _project(xs, w_in[0], 1, ns * steps, _tile(ns * steps, 256), False)
    mo_s, c_s, n_s, m_s = _mlstm_sample(ps, bias, mlstm_norm_g[0], state_C[0], state_n[0], state_m[0], ns, steps)
    ao_s = _dsa_sample(ps, cache_k[0], cache_v[0], cache_k_idx[0], page_table, rel_bias, ns, steps)
    y_s = _finish(xs, mo_s, ao_s, ffn, _tile(ns * steps, 512), _tile(ns * steps, 256))

    dt = x_prompt.dtype
    L = lambda a: a[None]
    heads = lambda a: jnp.transpose(a.reshape(nb, H_A, D_HA, seq), (0, 3, 1, 2))
    return (y_p.reshape(nb, seq, d), y_s.reshape(ns, steps, d),
            L(heads(pp["ktf"])), L(heads(pp["vtf"])), L(jnp.transpose(pp["iktf"], (0, 2, 1))),
            L(c_p.astype(dt)), L(n_p.astype(dt)), L(m_p.astype(dt)),
            L(ps["ak"].reshape(ns, steps, H_A, D_HA)), L(ps["av"].reshape(ns, steps, H_A, D_HA)),
            L(ps["ik"].reshape(ns, steps, D_I)),
            L(c_s.astype(state_C.dtype)), L(n_s.astype(state_n.dtype)), L(m_s.astype(state_m.dtype)))
```

```python
import functools
import math

import numpy as np
import jax
import jax.numpy as jnp
from jax import lax
from jax.experimental import pallas as pl
from jax.experimental.pallas import tpu as pltpu

F32, BF16, I32 = jnp.float32, jnp.bfloat16, jnp.int32

H_M, D_HM = 4, 128
H_A, D_HA = 8, 64
H_I, D_I = 4, 64
W_M, W_A = H_M * D_HM, H_A * D_HA
TOPK_MAX = 256
N_BUCKETS, MAX_DISTANCE = 32, 128
N_EXPERTS, TOP_K_EXP = 32, 4
SWIGLU_LIMIT, SWIGLU_ALPHA = 7.0, 1.702
GATE_CAP = 15.0
LN_EPS = 1e-5
DEPTH = 1
DN_ALPHA = (2.0 * DEPTH) ** 0.25

LANES = 128
VMEM_LIMIT_BYTES = 56 * 1024 * 1024
LOG2E = 1.4426950408889634
NEG = -1e30
NEG_BITS = int(np.float32(NEG).view(np.int32))
INT_MIN = -2 ** 31
KEY_CHUNK = 512
TAIL = KEY_CHUNK + LANES
ROW_BLOCK = 512
KV_SLOTS = 4
SMALL_IK, SMALL_MI, SMALL_MF, SMALL_IW = 0, 64, 68, 72

_NT = (((1,), (1,)), ((), ()))


def _cparams(sem):
    return pltpu.CompilerParams(dimension_semantics=sem, vmem_limit_bytes=VMEM_LIMIT_BYTES)


def _const_spec(shape):
    zeros = (0,) * len(shape)
    return pl.BlockSpec(shape, lambda *_: zeros, pipeline_mode=pl.Buffered(1))


_PROJ_COMMON = ("mq", "mk", "mv", "mo", "sm", "aq", "qs", "lo", "hi")
_PROJ_EXTRA = {
    True: ("avb", "ktb", "iktb", "ktf", "vtf", "iktf"),
    False: ("ak", "av", "ik"),
}


def _proj_kernel(x_ref, wm_ref, wa_ref, wi_ref, ws_ref, wt_ref, *out_refs, transposed):
    o = dict(zip(_PROJ_COMMON + _PROJ_EXTRA[transposed], out_refs))
    xb = x_ref[...].astype(BF16)
    ym = jnp.dot(xb, wm_ref[...], preferred_element_type=F32)
    o["mq"][...] = ym[:, 0:W_M].astype(BF16)
    o["mk"][...] = (ym[:, W_M:2 * W_M] * (D_HM ** -0.5)).astype(BF16)
    o["mv"][...] = ym[:, 2 * W_M:3 * W_M].astype(BF16)
    o["mo"][...] = ym[:, 3 * W_M:4 * W_M]
    ya = jnp.dot(xb, wa_ref[...], preferred_element_type=F32)
    o["aq"][...] = (ya[:, 0:W_A] * (D_HA ** -0.5 * (LOG2E if transposed else 1.0))).astype(BF16)
    xl = (x_ref[...] - xb.astype(F32)).astype(BF16)
    ys = (jnp.dot(xb, ws_ref[0], preferred_element_type=F32) + jnp.dot(xl, ws_ref[0], preferred_element_type=F32)
          + jnp.dot(xb, ws_ref[1], preferred_element_type=F32))
    o["sm"][...] = ys
    yi = jnp.dot(xb, wi_ref[...], preferred_element_type=F32)
    wsc = ys * (H_I ** -0.5 * D_I ** -0.5)
    lane = lax.broadcasted_iota(I32, yi.shape, 1)
    scale = wsc[:, SMALL_IW + 3:SMALL_IW + 4]
    for h in (2, 1, 0):
        scale = jnp.where(lane < (h + 1) * D_I, wsc[:, SMALL_IW + h:SMALL_IW + h + 1], scale)
    o["qs"][...] = (yi * scale).astype(BF16)
    o["lo"][...] = jnp.where(ys >= 0.0, 0.0, -jnp.inf)
    o["hi"][...] = jnp.where(ys <= 0.0, 0.0, jnp.inf)
    if transposed:
        o["avb"][...] = ya[:, W_A:2 * W_A].astype(BF16)
        yt = lax.dot_general(wt_ref[...], xb, _NT, preferred_element_type=F32)
        o["ktf"][...] = yt[0:W_A]
        o["ktb"][...] = yt[0:W_A].astype(BF16)
        o["vtf"][...] = yt[W_A:2 * W_A]
        o["iktf"][...] = yt[2 * W_A:2 * W_A + D_I]
        o["iktb"][...] = yt[2 * W_A:2 * W_A + D_I].astype(BF16)
    else:
        o["ak"][...] = ya[:, W_A:2 * W_A]
        o["av"][...] = ya[:, 2 * W_A:3 * W_A]
        o["ik"][...] = ys[:, SMALL_IK:SMALL_IK + D_I]


def _pack_w_in(w_in, transposed):
    d = w_in.shape[0]
    c = np.cumsum([0, W_M, W_M, W_M, W_M, H_M, H_M, W_A, W_A, W_A, H_I * D_I, D_I, H_I])
    seg = lambda i: w_in[:, c[i]:c[i + 1]]
    wm = w_in[:, c[0]:c[4]]
    wa = jnp.concatenate([seg(6), seg(8)], axis=1) if transposed else w_in[:, c[6]:c[9]]
    wi = seg(9)
    ws = jnp.concatenate([seg(10), seg(4), seg(5), seg(11),
                          jnp.zeros((d, LANES - D_I - 2 * H_M - H_I), w_in.dtype)], axis=1)
    wt = jnp.concatenate([seg(7), seg(8), seg(10)], axis=1).T
    ws_hi = ws.astype(BF16)
    ws2 = jnp.stack([ws_hi, (ws - ws_hi.astype(F32)).astype(BF16)])
    return (wm.astype(BF16), wa.astype(BF16), wi.astype(BF16), ws2, wt.astype(BF16))


def _project(x2, w_in, nb, seq, tm, transposed):
    n, d = x2.shape
    packs = _pack_w_in(w_in, transposed)
    nps = seq // tm
    sds = jax.ShapeDtypeStruct
    row = lambda w, dt: (sds((n, w), dt), pl.BlockSpec((tm, w), lambda i: (i, 0)))
    tr = lambda w, dt: (sds((nb, w, seq), dt), pl.BlockSpec((None, w, tm), lambda i: (i // nps, 0, i % nps)))
    outs = dict(
        mq=row(W_M, BF16), mk=row(W_M, BF16), mv=row(W_M, BF16), mo=row(W_M, F32), sm=row(LANES, F32),
        aq=row(W_A, BF16), qs=row(H_I * D_I, BF16), lo=row(LANES, F32), hi=row(LANES, F32),
        avb=row(W_A, BF16), ktb=tr(W_A, BF16), iktb=tr(D_I, BF16), ktf=tr(W_A, F32), vtf=tr(W_A, F32),
        iktf=tr(D_I, F32), ak=row(W_A, F32), av=row(W_A, F32), ik=row(D_I, F32),
    )
    names = _PROJ_COMMON + _PROJ_EXTRA[transposed]
    res = pl.pallas_call(
        functools.partial(_proj_kernel, transposed=transposed),
        grid=(n // tm,),
        in_specs=[pl.BlockSpec((tm, d), lambda i: (i, 0))] + [_const_spec(w.shape) for w in packs],
        out_specs=[outs[k][1] for k in names],
        out_shape=[outs[k][0] for k in names],
        compiler_params=_cparams(("arbitrary",)),
        name="proj_t" if transposed else "proj",
    )(x2, *packs)
    return dict(zip(names, res))


def _gates(sm, bias):
    cap = GATE_CAP * jnp.tanh((sm + bias) / GATE_CAP)
    e = jnp.exp(-jnp.abs(cap))
    u = 1.0 + e
    log1p_e = jnp.where(u == 1.0, e, jnp.log(u) * (e / (u - 1.0)))
    logsig = jnp.minimum(cap, 0.0) - log1p_e
    lane = lax.broadcasted_iota(I32, sm.shape, 1)
    return jnp.where(lane >= SMALL_MF, logsig, cap)


def _split3(x):
    a = x.astype(BF16)
    r = x - a.astype(F32)
    b = r.astype(BF16)
    c = (r - b.astype(F32)).astype(BF16)
    return a, b, c


def _dot3(mat, parts):
    return sum(jnp.dot(mat, p, preferred_element_type=F32) for p in parts)


def _headnorm_out(hh, mo, g):
    mu = jnp.mean(hh, axis=1, keepdims=True)
    var = jnp.mean(jnp.square(hh - mu), axis=1, keepdims=True)
    hn = (hh - mu) * lax.rsqrt(var + LN_EPS) * g
    return (hn / (1.0 + jnp.exp(-mo))).astype(BF16)


def _mlstm_prompt_kernel(q_ref, k_ref, v_ref, mo_ref, sm_ref, bias_ref, g_ref,
                         out_ref, c_out, n_out, m_out, c_s, n_s, m_s, *, chunk):
    j = pl.program_id(1)

    @pl.when(j == 0)
    def _():
        c_s[...] = jnp.zeros_like(c_s)
        n_s[...] = jnp.zeros_like(n_s)
        m_s[...] = jnp.zeros_like(m_s)

    L = chunk
    gate = _gates(sm_ref[...], bias_ref[...])
    r_i = lax.broadcasted_iota(I32, (L, L), 0)
    c_i = lax.broadcasted_iota(I32, (L, L), 1)
    tril = c_i <= r_i
    cum = _dot3(jnp.where(tril, 1.0, 0.0).astype(BF16), _split3(gate))
    gate_t = gate.T
    cum_t = cum.T
    for h in range(H_M):
        sl = slice(h * D_HM, (h + 1) * D_HM)
        b_col = cum[:, SMALL_MF + h:SMALL_MF + h + 1]
        b_row = cum_t[SMALL_MF + h:SMALL_MF + h + 1, :]
        i_col = gate[:, SMALL_MI + h:SMALL_MI + h + 1]
        i_row = gate_t[SMALL_MI + h:SMALL_MI + h + 1, :]
        m_prev = m_s[h][:, 0:1]
        dmat = jnp.where(tril, b_col - b_row + i_row, -jnp.inf)
        inter = b_col + m_prev
        mt = jnp.maximum(inter, jnp.max(dmat, axis=1, keepdims=True))
        w_intra = jnp.exp(dmat - mt)
        w_inter = jnp.exp(inter - mt)
        q, k, v = q_ref[:, sl], k_ref[:, sl], v_ref[:, sl]
        s = lax.dot_general(q, k, _NT, preferred_element_type=F32) * w_intra
        c_old = c_s[h]
        n_old = n_s[h]
        num = (jnp.dot(s.astype(BF16), v, preferred_element_type=F32)
               + w_inter * lax.dot_general(q, c_old.astype(BF16), _NT, preferred_element_type=F32))
        den = (jnp.sum(s, axis=1, keepdims=True)
               + w_inter * jnp.sum(q.astype(F32) * n_old, axis=1, keepdims=True))
        hh = num / jnp.maximum(jnp.abs(den), jnp.exp(-mt))
        out_ref[:, sl] = _headnorm_out(hh, mo_ref[:, sl], g_ref[:, sl])
        bl = b_row[:, L - 1:L]
        decay = bl - b_col + i_col
        m_new = jnp.maximum(bl + m_prev, jnp.max(decay, axis=0, keepdims=True))
        ws = jnp.exp(decay - m_new)
        cs = jnp.exp(bl + m_prev - m_new)
        vw_t = (v.astype(F32) * ws).T.astype(BF16)
        c_s[h] = cs * c_old + jnp.dot(vw_t, k, preferred_element_type=F32)
        n_s[h] = cs * n_old + jnp.sum(k.astype(F32) * ws, axis=0, keepdims=True)
        m_s[h] = jnp.broadcast_to(m_new, (1, LANES))

    @pl.when(j == pl.num_programs(1) - 1)
    def _():
        c_out[...] = c_s[...]
        n_out[...] = n_s[...]
        m_out[...] = m_s[...]


def _gate_bias(b_ig, b_fg):
    z = jnp.zeros((LANES,), F32)
    z = z.at[SMALL_MI:SMALL_MI + H_M].set(b_ig.astype(F32)).at[SMALL_MF:SMALL_MF + H_M].set(b_fg.astype(F32))
    return z.reshape(1, LANES)


def _mlstm_prompt(pr, bias, norm_g, nb, seq, chunk):
    n = nb * seq
    nc = seq // chunk
    row = lambda w: pl.BlockSpec((chunk, w), lambda b, j: (b * nc + j, 0))
    st = lambda *s: pl.BlockSpec((None,) + s, lambda b, j: (b,) + (0,) * len(s))
    sds = jax.ShapeDtypeStruct
    out, c, nn, m = pl.pallas_call(
        functools.partial(_mlstm_prompt_kernel, chunk=chunk),
        grid=(nb, nc),
        in_specs=[row(W_M), row(W_M), row(W_M), row(W_M), row(LANES),
                  pl.BlockSpec((1, LANES), lambda b, j: (0, 0)), pl.BlockSpec((1, W_M), lambda b, j: (0, 0))],
        out_specs=[row(W_M), st(H_M, D_HM, D_HM), st(H_M, 1, D_HM), st(H_M, 1, LANES)],
        out_shape=[sds((n, W_M), BF16), sds((nb, H_M, D_HM, D_HM), F32),
                   sds((nb, H_M, 1, D_HM), F32), sds((nb, H_M, 1, LANES), F32)],
        scratch_shapes=[pltpu.VMEM((H_M, D_HM, D_HM), F32), pltpu.VMEM((H_M, 1, D_HM), F32),
                        pltpu.VMEM((H_M, 1, LANES), F32)],
        compiler_params=_cparams(("arbitrary", "arbitrary")),
        name="mlstm_prompt",
    )(pr["mq"], pr["mk"], pr["mv"], pr["mo"], pr["sm"], bias, norm_g.reshape(1, W_M).astype(F32))
    return out, c, nn.reshape(nb, H_M, D_HM), m[:, :, 0, 0]


def _mlstm_sample_kernel(q_ref, k_ref, v_ref, mo_ref, sm_ref, bias_ref, g_ref, mrow_ref, nrow_ref,
                         c_in, n_in, out_ref, c_out, n_out, m_out, numi_s, *, steps):
    h = pl.program_id(1)
    R = q_ref.shape[0]
    nseq = R // steps
    gate = _gates(sm_ref[...], bias_ref[...])
    r_i = lax.broadcasted_iota(I32, (R, R), 0)
    c_i = lax.broadcasted_iota(I32, (R, R), 1)
    same = (r_i // steps) == (c_i // steps)
    mask = jnp.logical_and(same, c_i <= r_i)
    parts = _split3(gate)
    cum = _dot3(jnp.where(mask, 1.0, 0.0).astype(BF16), parts)
    tot = _dot3(jnp.where(same, 1.0, 0.0).astype(BF16), parts)
    lane = lax.broadcasted_iota(I32, (R, LANES), 1)
    sub = lax.broadcasted_iota(I32, (LANES, R), 0)
    col = lambda a, off: jnp.sum(jnp.where(lane == off + h, a, 0.0), axis=1, keepdims=True)
    rowv = lambda a, off: jnp.sum(jnp.where(sub == off + h, a.T, 0.0), axis=0, keepdims=True)
    b_col, b_row = col(cum, SMALL_MF), rowv(cum, SMALL_MF)
    i_col, i_row = col(gate, SMALL_MI), rowv(gate, SMALL_MI)
    bl_col, bl_row = col(tot, SMALL_MF), rowv(tot, SMALL_MF)
    m_col = jnp.sum(jnp.where(lane == h, mrow_ref[...], 0.0), axis=1, keepdims=True)
    dmat = jnp.where(mask, b_col - b_row + i_row, -jnp.inf)
    inter = b_col + m_col
    mt = jnp.maximum(inter, jnp.max(dmat, axis=1, keepdims=True))
    w_intra = jnp.exp(dmat - mt)
    w_inter = jnp.exp(inter - mt)
    q, k, v = q_ref[...], k_ref[...], v_ref[...]
    s = lax.dot_general(q, k, _NT, preferred_element_type=F32) * w_intra
    for b in range(nseq):
        rows = slice(b * steps, (b + 1) * steps)
        numi_s[rows, :] = lax.dot_general(q[rows, :], c_in[b].astype(BF16), _NT, preferred_element_type=F32)
    num = jnp.dot(s.astype(BF16), v, preferred_element_type=F32) + w_inter * numi_s[...]
    den = (jnp.sum(s, axis=1, keepdims=True)
           + w_inter * jnp.sum(q.astype(F32) * nrow_ref[...], axis=1, keepdims=True))
    hh = num / jnp.maximum(jnp.abs(den), jnp.exp(-mt))
    out_ref[...] = _headnorm_out(hh, mo_ref[...], g_ref[...])
    decay_col = bl_col - b_col + i_col
    decay_row = bl_row - b_row + i_row
    seg_max = jnp.max(jnp.where(same, decay_row, -jnp.inf), axis=1, keepdims=True)
    m_new = jnp.maximum(bl_col + m_col, seg_max)
    ws = jnp.exp(decay_col - m_new)
    cs = jnp.exp(bl_col + m_col - m_new)
    kf = k.astype(F32)
    vw_t = (v.astype(F32) * ws).T.astype(BF16)
    kw = kf * ws
    rid = lax.broadcasted_iota(I32, (R, 1), 0) // steps
    for b in range(nseq):
        r0 = b * steps
        kb = jnp.where(rid == b, kf, 0.0).astype(BF16)
        cs_b = cs[r0:r0 + 1, :]
        c_out[b] = cs_b * c_in[b] + jnp.dot(vw_t, kb, preferred_element_type=F32)
        n_out[b] = cs_b * n_in[b] + jnp.sum(kw[r0:r0 + steps, :], axis=0, keepdims=True)
        m_out[b] = jnp.broadcast_to(m_new[r0:r0 + 1, :], (1, LANES))


def _mlstm_sample(pr, bias, norm_g, state_c, state_n, state_m, nseq_total, steps):
    n = nseq_total * steps
    R = LANES
    spt = R // steps
    m_rows = jnp.pad(jnp.repeat(state_m.astype(F32), steps, axis=0), ((0, 0), (0, LANES - H_M)))
    n_rows = jnp.repeat(state_n.astype(F32).reshape(nseq_total, W_M), steps, axis=0)
    n4 = state_n.astype(F32).reshape(nseq_total, H_M, 1, D_HM)
    colh = lambda w: pl.BlockSpec((R, w), lambda i, h: (i, h))
    sth = lambda *s: pl.BlockSpec((spt, None) + s, lambda i, h: (i, h) + (0,) * len(s))
    sds = jax.ShapeDtypeStruct
    out, c, nn, m = pl.pallas_call(
        functools.partial(_mlstm_sample_kernel, steps=steps),
        grid=(n // R, H_M),
        in_specs=[colh(D_HM), colh(D_HM), colh(D_HM), colh(D_HM),
                  pl.BlockSpec((R, LANES), lambda i, h: (i, 0)),
                  pl.BlockSpec((1, LANES), lambda i, h: (0, 0)),
                  pl.BlockSpec((1, D_HM), lambda i, h: (0, h)),
                  pl.BlockSpec((R, LANES), lambda i, h: (i, 0)),
                  colh(D_HM), sth(D_HM, D_HM), sth(1, D_HM)],
        out_specs=[colh(D_HM), sth(D_HM, D_HM), sth(1, D_HM), sth(1, LANES)],
        out_shape=[sds((n, W_M), BF16), sds((nseq_total, H_M, D_HM, D_HM), F32),
                   sds((nseq_total, H_M, 1, D_HM), F32), sds((nseq_total, H_M, 1, LANES), F32)],
        scratch_shapes=[pltpu.VMEM((R, D_HM), F32)],
        compiler_params=_cparams(("arbitrary", "arbitrary")),
        name="mlstm_sample",
    )(pr["mq"], pr["mk"], pr["mv"], pr["mo"], pr["sm"], bias, norm_g.reshape(1, W_M).astype(F32),
      m_rows, n_rows, state_c.astype(F32), n4)
    return out, c, nn.reshape(nseq_total, H_M, D_HM), m[:, :, 0, 0]


def _rel_bucket(dist):
    max_exact = N_BUCKETS // 2
    d = jnp.maximum(dist, 0)
    df = jnp.maximum(d.astype(F32), 1.0)
    large = max_exact + (jnp.log(df / max_exact) / math.log(MAX_DISTANCE / max_exact)
                         * (N_BUCKETS - max_exact)).astype(I32)
    large = jnp.minimum(large, N_BUCKETS - 1)
    return jnp.where(d < max_exact, d, large)


def _bias_delta(rel_bias, dist):
    far = rel_bias[_rel_bucket(jnp.asarray(4 * MAX_DISTANCE, I32))]
    return (rel_bias[_rel_bucket(dist)] - far).astype(F32)


def _sortable(score):
    score = jnp.where(score == 0.0, 0.0, score)
    bits = pltpu.bitcast(score, I32)
    return bits ^ ((bits >> 31) & 0x7FFFFFFF)


def _kth_largest_search(count_ge, total, short, topk, rows):
    def cond(st):
        bit, _, _, done = st
        return jnp.logical_and(bit >= 0, jnp.min(done) < 0.5)

    def body(st):
        bit, pref, cnt, done = st
        cand_u = pref | lax.shift_left(jnp.int32(1), bit)
        c = count_ge(cand_u ^ INT_MIN)
        take = jnp.logical_and(c >= topk, done < 0.5)
        pref = jnp.where(take, cand_u, pref)
        cnt = jnp.where(take, c, cnt)
        done = jnp.where(cnt == topk, 1.0, done)
        return bit - 1, pref, cnt, done

    c_nonneg = count_ge(jnp.zeros((rows, 1), I32))
    c_pos = count_ge(jnp.ones((rows, 1), I32))
    nonneg = c_nonneg >= topk
    zero_thr = jnp.logical_and(nonneg, c_pos < topk)
    pref0 = jnp.where(nonneg, INT_MIN, 0)
    cnt0 = jnp.where(nonneg, c_nonneg, total)
    done0 = jnp.logical_or(jnp.logical_or(short, zero_thr), cnt0 == topk)
    init = (jnp.int32(30), pref0, cnt0, jnp.where(done0, 1.0, 0.0))
    _, pref, cnt, _ = lax.while_loop(cond, body, init)
    thr = jnp.where(short, INT_MIN + 1, pref ^ INT_MIN)
    return thr, jnp.logical_and(jnp.logical_not(short), cnt != topk)


def _dsa_prompt_kernel(qs_ref, lo_ref, hi_ref, aq_ref, ikt_ref, kt_ref, v_ref, dz_ref, dzmax_ref, o_ref,
                       buf_ref, q2_ref, bnd_ref, kmax_ref, m_ref, l_ref, ls2_ref, acc2_ref, *, tq, topk):
    CH = KEY_CHUNK
    q0 = pl.program_id(1) * tq
    nchunk = jnp.maximum(lax.shift_right_logical(q0 + tq + CH - 1, 9), TAIL // CH + 1)

    @pl.when(pl.program_id(1) == 0)
    def _():
        def body(c, mx):
            kc = kt_ref[:, pl.ds(pl.multiple_of(c * CH, CH), CH)].astype(F32)
            sq = kc * kc
            parts = [jnp.sum(sq[h * D_HA:(h + 1) * D_HA], axis=0, keepdims=True) for h in range(H_A)]
            return jnp.maximum(mx, jnp.concatenate(parts, axis=0))
        mx = lax.fori_loop(0, kt_ref.shape[1] // CH, body, jnp.zeros((H_A, CH), F32))
        km = jnp.sqrt(jnp.max(mx, axis=1, keepdims=True))
        for h in range(H_A):
            kmax_ref[h] = jnp.broadcast_to(km[h:h + 1, :], (1, LANES))

    qs = qs_ref[...]
    lo = lo_ref[...]
    hi = hi_ref[...]
    qs_hm = jnp.concatenate([qs[:, h * D_I:(h + 1) * D_I] for h in range(H_I)], axis=0)
    lo_hm = jnp.concatenate([lo[:, SMALL_IW + h:SMALL_IW + h + 1] for h in range(H_I)], axis=0)
    hi_hm = jnp.concatenate([hi[:, SMALL_IW + h:SMALL_IW + h + 1] for h in range(H_I)], axis=0)
    qpos = q0 + lax.broadcasted_iota(I32, (tq, CH), 0)
    lane_ch = lax.broadcasted_iota(I32, (tq, CH), 1)

    def score_chunk(c, carry):
        st = pl.multiple_of(c * CH, CH)
        a = jnp.dot(qs_hm, ikt_ref[:, pl.ds(st, CH)], preferred_element_type=F32)
        t = jnp.maximum(jnp.minimum(a, hi_hm), lo_hm)
        sc = sum(t[h * tq:(h + 1) * tq] for h in range(H_I))
        key = jnp.where(st + lane_ch <= qpos, _sortable(sc), INT_MIN)
        buf_ref[:, pl.ds(st, CH)] = key
        return carry

    lax.fori_loop(0, nchunk, score_chunk, 0)

    def count_ge(cand):
        candb = jnp.broadcast_to(cand, (tq, LANES))

        def body(c, acc):
            st = c * CH
            for g in range(CH // LANES):
                blk = buf_ref[:, pl.ds(pl.multiple_of(st + g * LANES, LANES), LANES)]
                acc = acc + jnp.where(blk >= candb, 1.0, 0.0)
            return acc

        acc = lax.fori_loop(0, nchunk, body, jnp.zeros((tq, LANES), F32))
        return jnp.sum(acc, axis=1, keepdims=True)

    nvalid = q0 + 1 + lax.broadcasted_iota(I32, (tq, 1), 0)
    short = nvalid <= topk
    thr, tie = _kth_largest_search(count_ge, (nchunk * CH).astype(F32), short, topk, tq)
    any_tie = jnp.max(jnp.where(tie, 1.0, 0.0)) > 0.5

    @pl.when(jnp.logical_not(any_tie))
    def _():
        def fin(c, carry):
            st = pl.multiple_of(c * CH, CH)
            blk = buf_ref[:, pl.ds(st, CH)]
            buf_ref[:, pl.ds(st, CH)] = jnp.where(blk >= thr, 0, NEG_BITS)
            return carry
        lax.fori_loop(0, nchunk, fin, 0)

    @pl.when(any_tie)
    def _():
        need = topk - count_ge(thr + 1)
        upper = (lax.broadcasted_iota(I32, (CH, CH), 0) < lax.broadcasted_iota(I32, (CH, CH), 1))
        upper = jnp.where(upper, 1.0, 0.0).astype(BF16)

        def fin(c, seen):
            st = pl.multiple_of(c * CH, CH)
            blk = buf_ref[:, pl.ds(st, CH)]
            eq = jnp.where(blk == thr, 1.0, 0.0)
            before = jnp.dot(eq.astype(BF16), upper, preferred_element_type=F32) + seen
            sel = jnp.logical_or(blk > thr, jnp.logical_and(blk == thr, before < need))
            buf_ref[:, pl.ds(st, CH)] = jnp.where(sel, 0, NEG_BITS)
            return seen + jnp.sum(eq, axis=1, keepdims=True)
        lax.fori_loop(0, nchunk, fin, jnp.zeros((tq, 1), F32))

    lane = lax.broadcasted_iota(I32, (tq, LANES), 1)
    left = lane < D_HA
    for hp in range(H_A // 2):
        aqp = aq_ref[:, hp * LANES:(hp + 1) * LANES]
        zero = jnp.zeros_like(aqp)
        q2_ref[hp, 0:tq, :] = jnp.where(left, aqp, zero)
        q2_ref[hp, tq:2 * tq, :] = jnp.where(left, zero, aqp)
    ts = jnp.maximum(q0 - CH, 0)
    nfar = lax.shift_right_logical(ts + CH - 1, 9)
    lim = nfar * CH - ts
    covered = jnp.where(lax.broadcasted_iota(I32, (tq, TAIL), 1) < lim, NEG, 0.0)
    dz_start = pl.multiple_of(CH - (q0 - ts), LANES)
    two = lambda a: jnp.concatenate([a, a], axis=0)

    def logits(hp, st, width, mb2, with_dz):
        kt = kt_ref[hp * LANES:(hp + 1) * LANES, pl.ds(st, width)]
        s = jnp.dot(q2_ref[hp], kt, preferred_element_type=F32) + mb2
        if with_dz:
            s = s + jnp.concatenate([dz_ref[2 * hp, :, pl.ds(dz_start, width)],
                                     dz_ref[2 * hp + 1, :, pl.ds(dz_start, width)]], axis=0)
        return s

    def mask_bias(st, width, extra):
        mb = pltpu.bitcast(buf_ref[:, pl.ds(st, width)], F32)
        return two(mb if extra is None else mb + extra)

    def lane_groups(p):
        return sum(p[:, g * LANES:(g + 1) * LANES] for g in range(p.shape[1] // LANES))

    for hp in range(H_A // 2):
        q2 = q2_ref[hp].astype(F32)
        qn = jnp.sqrt(jnp.sum(q2 * q2, axis=1, keepdims=True))
        r2 = lax.broadcasted_iota(I32, (2 * tq, 1), 0)
        kmax = jnp.where(r2 < tq, kmax_ref[2 * hp][:, 0:1], kmax_ref[2 * hp + 1][:, 0:1])
        dzmax = jnp.where(r2 < tq, dzmax_ref[2 * hp][:, 0:1], dzmax_ref[2 * hp + 1][:, 0:1])
        bnd_ref[hp] = qn * kmax + dzmax
    acc2_ref[...] = jnp.zeros(acc2_ref.shape, F32)
    ls2_ref[...] = jnp.zeros(ls2_ref.shape, F32)

    def attend_fast(st, width, extra, with_dz):
        mb2 = mask_bias(st, width, extra)
        for hp in range(H_A // 2):
            p = jnp.exp2(logits(hp, st, width, mb2, with_dz) - bnd_ref[hp])
            ls2_ref[hp] = ls2_ref[hp] + lane_groups(p)
            vv = v_ref[pl.ds(st, width), hp * LANES:(hp + 1) * LANES]
            acc2_ref[hp] = acc2_ref[hp] + jnp.dot(p.astype(BF16), vv, preferred_element_type=F32)

    def far_fast(jc, carry):
        attend_fast(pl.multiple_of(jc * CH, CH), CH, None, False)
        return carry

    lax.fori_loop(0, nfar, far_fast, 0)
    attend_fast(pl.multiple_of(ts, LANES), TAIL, covered, True)
    lmin = jnp.float32(jnp.inf)
    for hp in range(H_A // 2):
        l2 = jnp.sum(ls2_ref[hp], axis=1, keepdims=True)
        lmin = jnp.minimum(lmin, jnp.min(l2))
        acc = acc2_ref[hp]
        o_ref[:, hp * LANES:(hp + 1) * LANES] = jnp.where(left, acc[0:tq] / l2[0:tq], acc[tq:2 * tq] / l2[tq:2 * tq])
    underflow = jnp.logical_not(lmin > 1e-30)

    @pl.when(underflow)
    def _():
        m_ref[...] = jnp.full(m_ref.shape, NEG, F32)
        l_ref[...] = jnp.zeros(l_ref.shape, F32)
        acc2_ref[...] = jnp.zeros(acc2_ref.shape, F32)

        def attend_exact(st, width, extra, with_dz):
            mb2 = mask_bias(st, width, extra)
            for hp in range(H_A // 2):
                s = logits(hp, st, width, mb2, with_dz)
                m_old = m_ref[hp]
                m_new = jnp.maximum(m_old, jnp.max(s, axis=1, keepdims=True))
                alpha = jnp.exp2(m_old - m_new)
                p = jnp.exp2(s - m_new)
                l_ref[hp] = alpha * l_ref[hp] + jnp.sum(p, axis=1, keepdims=True)
                m_ref[hp] = m_new
                vv = v_ref[pl.ds(st, width), hp * LANES:(hp + 1) * LANES]
                acc2_ref[hp] = alpha * acc2_ref[hp] + jnp.dot(p.astype(BF16), vv, preferred_element_type=F32)

        def far_exact(jc, carry):
            attend_exact(pl.multiple_of(jc * CH, CH), CH, None, False)
            return carry

        lax.fori_loop(0, nfar, far_exact, 0)
        attend_exact(pl.multiple_of(ts, LANES), TAIL, covered, True)
        for hp in range(H_A // 2):
            acc = acc2_ref[hp]
            l2 = l_ref[hp]
            o_ref[:, hp * LANES:(hp + 1) * LANES] = jnp.where(left, acc[0:tq] / l2[0:tq],
                                                              acc[tq:2 * tq] / l2[tq:2 * tq])


def _dsa_prompt(pr, rel_bias, nb, seq, tq):
    topk = min(TOPK_MAX, seq // 4)
    assert seq % KEY_CHUNK == 0 and seq >= TAIL + KEY_CHUNK and tq == LANES
    wtab = KEY_CHUNK + TAIL
    dmin, dmax = KEY_CHUNK - (wtab - 1), KEY_CHUNK + tq - 1
    g = (_bias_delta(rel_bias, jnp.arange(dmax, dmin - 1, -1, dtype=I32)) * LOG2E).T
    dz = jnp.stack([g[:, tq - 1 - r:tq - 1 - r + wtab] for r in range(tq)], axis=1)
    dzmax = jnp.broadcast_to(jnp.max(g, axis=1).reshape(H_A, 1, 1), (H_A, 1, LANES))
    nq = seq // tq
    rowq = lambda w: pl.BlockSpec((tq, w), lambda b, i: (b * nq + i, 0))
    perb = lambda r, c: pl.BlockSpec((None, r, c), lambda b, i: (b, 0, 0), pipeline_mode=pl.Buffered(1))
    vb = pr["avb"].reshape(nb, seq, W_A)
    return pl.pallas_call(
        functools.partial(_dsa_prompt_kernel, tq=tq, topk=topk),
        grid=(nb, nq),
        in_specs=[rowq(H_I * D_I), rowq(LANES), rowq(LANES), rowq(W_A),
                  perb(D_I, seq), perb(W_A, seq), perb(seq, W_A), _const_spec((H_A, tq, wtab)),
                  _const_spec((H_A, 1, LANES))],
        out_specs=rowq(W_A),
        out_shape=jax.ShapeDtypeStruct((nb * seq, W_A), F32),
        scratch_shapes=[pltpu.VMEM((tq, seq), I32),
                        pltpu.VMEM((H_A // 2, 2 * tq, LANES), BF16),
                        pltpu.VMEM((H_A // 2, 2 * tq, 1), F32),
                        pltpu.VMEM((H_A, 1, LANES), F32),
                        pltpu.VMEM((H_A // 2, 2 * tq, 1), F32), pltpu.VMEM((H_A // 2, 2 * tq, 1), F32),
                        pltpu.VMEM((H_A // 2, 2 * tq, LANES), F32),
                        pltpu.VMEM((H_A // 2, 2 * tq, LANES), F32)],
        compiler_params=_cparams(("arbitrary", "arbitrary")),
        name="dsa_prompt",
    )(pr["qs"], pr["lo"], pr["hi"], pr["aq"], pr["iktb"], pr["ktb"], vb, dz, dzmax)


def _dsa_sample_kernel(pt_ref, qs_ref, lo_ref, hi_ref, aq_ref, ikn_ref, akn_ref, avn_ref, dz_ref,
                       cki_ref, ck_ref, cv_ref, o_ref,
                       kib, kbuf, vbuf, buf_ref, sem_ki, sem_kv, *, steps, topk, n_pages, page, cpp):
    b = pl.program_id(0)
    nb = pl.num_programs(0)
    past = n_pages * page
    nch = n_pages // cpp
    ckeys = cpp * page
    width = past + LANES
    rows_a = H_A * steps

    def ki_copy(seq_i, slot, p):
        return pltpu.make_async_copy(cki_ref.at[pt_ref[seq_i, p]], kib.at[slot, p], sem_ki.at[slot])

    def kv_copies(seq_i, c, slot, p):
        pg = pt_ref[seq_i, c * cpp + p]
        return (pltpu.make_async_copy(ck_ref.at[pg], kbuf.at[slot, p], sem_kv.at[slot]),
                pltpu.make_async_copy(cv_ref.at[pg], vbuf.at[slot, p], sem_kv.at[slot]))

    def start_ki(seq_i, slot):
        def f(p, carry):
            ki_copy(seq_i, slot, p).start()
            return carry
        lax.fori_loop(0, n_pages, f, 0)

    n_slots = kbuf.shape[0]
    ahead = n_slots - 1
    total_chunks = nb * nch

    def start_kv(g):
        @pl.when(g < total_chunks)
        def _():
            def f(p, carry):
                for cp in kv_copies(g // nch, g % nch, g % n_slots, p):
                    cp.start()
                return carry
            lax.fori_loop(0, cpp, f, 0)

    def wait_kv(g):
        def f(p, carry):
            for cp in kv_copies(g // nch, g % nch, g % n_slots, p):
                cp.wait()
            return carry
        lax.fori_loop(0, cpp, f, 0)

    slot_b = b % 2

    @pl.when(b == 0)
    def _():
        start_ki(0, 0)
        for g0 in range(ahead):
            start_kv(jnp.int32(g0))

    def wait_ki(p, carry):
        ki_copy(b, slot_b, p).wait()
        return carry
    lax.fori_loop(0, n_pages, wait_ki, 0)

    @pl.when(b + 1 < nb)
    def _():
        start_ki(b + 1, 1 - slot_b)

    qs = qs_ref[...]
    lo = lo_ref[...]
    hi = hi_ref[...]
    qs_hm = jnp.concatenate([qs[:, h * D_I:(h + 1) * D_I] for h in range(H_I)], axis=0)
    lo_hm = jnp.concatenate([lo[:, SMALL_IW + h:SMALL_IW + h + 1] for h in range(H_I)], axis=0)
    hi_hm = jnp.concatenate([hi[:, SMALL_IW + h:SMALL_IW + h + 1] for h in range(H_I)], axis=0)

    def head_sum(a):
        t = jnp.maximum(jnp.minimum(a, hi_hm), lo_hm)
        return sum(t[h * steps:(h + 1) * steps] for h in range(H_I))

    def score_page(p, carry):
        a = jnp.dot(qs_hm, kib[slot_b, p].astype(BF16), preferred_element_type=F32)
        buf_ref[:, pl.ds(pl.multiple_of(p * page, page), page)] = _sortable(head_sum(a))
        return carry
    lax.fori_loop(0, n_pages, score_page, 0, unroll=4)

    ikn = jnp.concatenate([ikn_ref[...], jnp.zeros((LANES - steps, D_I), F32)], axis=0).astype(BF16)
    a_new = lax.dot_general(qs_hm, ikn, _NT, preferred_element_type=F32)
    t_i = lax.broadcasted_iota(I32, (steps, LANES), 0)
    j_i = lax.broadcasted_iota(I32, (steps, LANES), 1)
    buf_ref[:, past:width] = jnp.where(j_i <= t_i, _sortable(head_sum(a_new)), INT_MIN)

    def count_ge(cand):
        return jnp.sum(jnp.where(buf_ref[...] >= cand, 1.0, 0.0), axis=1, keepdims=True)

    nvalid = past + 1 + lax.broadcasted_iota(I32, (steps, 1), 0)
    short = nvalid <= topk
    thr, tie = _kth_largest_search(count_ge, jnp.float32(width), short, topk, steps)
    any_tie = jnp.max(jnp.where(tie, 1.0, 0.0)) > 0.5

    @pl.when(jnp.logical_not(any_tie))
    def _():
        buf_ref[...] = jnp.where(buf_ref[...] >= thr, 0, NEG_BITS)

    @pl.when(any_tie)
    def _():
        need = topk - count_ge(thr + 1)
        TW = LANES
        upper = (lax.broadcasted_iota(I32, (TW, TW), 0) < lax.broadcasted_iota(I32, (TW, TW), 1))
        upper = jnp.where(upper, 1.0, 0.0).astype(BF16)

        def fin(c, seen):
            st = pl.multiple_of(c * TW, TW)
            blk = buf_ref[:, pl.ds(st, TW)]
            eq = jnp.where(blk == thr, 1.0, 0.0)
            before = jnp.dot(eq.astype(BF16), upper, preferred_element_type=F32) + seen
            sel = jnp.logical_or(blk > thr, jnp.logical_and(blk == thr, before < need))
            buf_ref[:, pl.ds(st, TW)] = jnp.where(sel, 0, NEG_BITS)
            return seen + jnp.sum(eq, axis=1, keepdims=True)
        lax.fori_loop(0, width // TW, fin, jnp.zeros((steps, 1), F32))

    r_head = lax.broadcasted_iota(I32, (rows_a, W_A), 0) // steps
    c_head = lax.broadcasted_iota(I32, (rows_a, W_A), 1) // D_HA
    own = r_head == c_head
    aq_t = jnp.concatenate([aq_ref[...]] * H_A, axis=0)
    q_bd = jnp.where(own, aq_t, jnp.zeros_like(aq_t))
    tile_rows = lambda a: jnp.concatenate([a] * H_A, axis=0)

    def softmax_step(state, s, pv_fn):
        m_old, l_old, acc = state
        m_new = jnp.maximum(m_old, jnp.max(s, axis=1, keepdims=True))
        alpha = jnp.exp(m_old - m_new)
        p = jnp.exp(s - m_new)
        return m_new, alpha * l_old + jnp.sum(p, axis=1, keepdims=True), alpha * acc + pv_fn(p.astype(BF16))

    def chunk_step(c, state, last):
        g = b * nch + c
        slot = g % n_slots
        wait_kv(g)
        start_kv(g + ahead)

        s = jnp.concatenate([jnp.dot(q_bd, kbuf[slot, p].astype(BF16), preferred_element_type=F32)
                             for p in range(cpp)], axis=1)
        s = s + tile_rows(pltpu.bitcast(buf_ref[:, pl.ds(pl.multiple_of(c * ckeys, ckeys), ckeys)], F32))
        if last:
            s = s + jnp.concatenate([jnp.zeros((rows_a, ckeys - page), F32), dz_ref[:, 0:page]], axis=1)

        def pv(pb):
            return sum(lax.dot_general(pb[:, p * page:(p + 1) * page], vbuf[slot, p].astype(BF16), _NT,
                                       preferred_element_type=F32) for p in range(cpp))
        return softmax_step(state, s, pv)

    state = (jnp.full((rows_a, 1), NEG, F32), jnp.zeros((rows_a, 1), F32), jnp.zeros((rows_a, W_A), F32))
    state = lax.fori_loop(0, nch - 1, lambda c, st: chunk_step(c, st, False), state)
    state = chunk_step(nch - 1, state, True)
    pad = jnp.zeros((LANES - steps, W_A), F32)
    kn = jnp.concatenate([akn_ref[...], pad], axis=0).astype(BF16)
    vn = jnp.concatenate([avn_ref[...], pad], axis=0).astype(BF16)
    s_new = (lax.dot_general(q_bd, kn, _NT, preferred_element_type=F32)
             + tile_rows(pltpu.bitcast(buf_ref[:, past:width], F32)) + dz_ref[:, page:2 * page])
    _, l_fin, acc = softmax_step(state, s_new, lambda pb: jnp.dot(pb, vn, preferred_element_type=F32))
    accn = jnp.where(own, acc / l_fin, 0.0)
    o_ref[...] = sum(accn[h * steps:(h + 1) * steps] for h in range(H_A))


def _dsa_sample(pr, cache_k, cache_v, cache_k_idx, page_table, rel_bias, nseq, steps):
    n_pool, page = cache_k.shape[0], cache_k.shape[1]
    n_pages = page_table.shape[1]
    past = n_pages * page
    topk = min(TOPK_MAX, (past + steps) // 4)
    cpp = 8 if n_pages % 8 == 0 else n_pages
    assert page == LANES and steps <= LANES
    t = jnp.arange(steps, dtype=I32)[:, None]
    c = jnp.arange(LANES, dtype=I32)[None, :]
    dz = jnp.transpose(_bias_delta(rel_bias, jnp.concatenate([t + LANES - c, t - c], axis=1)), (2, 0, 1))
    dz = dz.reshape(H_A * steps, 2 * LANES)
    ck_t = jnp.transpose(cache_k, (0, 2, 3, 1)).reshape(n_pool, W_A, page)
    cv_t = jnp.transpose(cache_v, (0, 2, 3, 1)).reshape(n_pool, W_A, page)
    cki_t = jnp.transpose(cache_k_idx, (0, 2, 1))
    rows = lambda w: pl.BlockSpec((steps, w), lambda b, pt: (b, 0))
    anyspec = pl.BlockSpec(memory_space=pl.ANY)
    return pl.pallas_call(
        functools.partial(_dsa_sample_kernel, steps=steps, topk=topk, n_pages=n_pages, page=page, cpp=cpp),
        grid_spec=pltpu.PrefetchScalarGridSpec(
            num_scalar_prefetch=1,
            grid=(nseq,),
            in_specs=[rows(H_I * D_I), rows(LANES), rows(LANES), rows(W_A), rows(D_I), rows(W_A), rows(W_A),
                      pl.BlockSpec((H_A * steps, 2 * LANES), lambda b, pt: (0, 0)),
                      anyspec, anyspec, anyspec],
            out_specs=rows(W_A),
            scratch_shapes=[pltpu.VMEM((2, n_pages, D_I, page), F32),
                            pltpu.VMEM((KV_SLOTS, cpp, W_A, page), F32), pltpu.VMEM((KV_SLOTS, cpp, W_A, page), F32),
                            pltpu.VMEM((steps, past + LANES), I32),
                            pltpu.SemaphoreType.DMA((2,)), pltpu.SemaphoreType.DMA((KV_SLOTS,))],
        ),
        out_shape=jax.ShapeDtypeStruct((nseq * steps, W_A), F32),
        compiler_params=_cparams(("arbitrary",)),
        name="dsa_sample",
    )(page_table.astype(I32), pr["qs"], pr["lo"], pr["hi"], pr["aq"], pr["ik"], pr["ak"], pr["av"], dz,
      cki_t, ck_t, cv_t)


def _layer_norm(y, g, b):
    mu = jnp.mean(y, axis=1, keepdims=True)
    var = jnp.mean(jnp.square(y - mu), axis=1, keepdims=True)
    return (y - mu) * lax.rsqrt(var + LN_EPS) * g + b


def _mix_router_kernel(x_ref, mo_ref, ao_ref, wo1_ref, wo2_ref, g_ref, b_ref, rw_ref, rb_ref,
                       h_ref, route_ref, gate_ref, cnt_ref, carry_s):
    @pl.when(pl.program_id(0) == 0)
    def _():
        carry_s[...] = jnp.zeros_like(carry_s)

    tm = x_ref.shape[0]
    mix = (jnp.dot(mo_ref[...], wo1_ref[...], preferred_element_type=F32)
           + jnp.dot(ao_ref[...].astype(BF16), wo2_ref[...], preferred_element_type=F32))
    h = _layer_norm(DN_ALPHA * x_ref[...] + mix, g_ref[...], b_ref[...])
    h_ref[...] = h
    logits = jnp.dot(h.astype(BF16), rw_ref[...], preferred_element_type=F32) + rb_ref[...]
    lane = lax.broadcasted_iota(I32, (tm, LANES), 1)
    lane_f = lane.astype(F32)
    l = jnp.where(lane < N_EXPERTS, logits, -jnp.inf)
    vals, hots, idxs = [], [], []
    member = jnp.zeros((tm, LANES), F32)
    for _ in range(TOP_K_EXP):
        mx = jnp.max(l, axis=1, keepdims=True)
        idx = jnp.min(jnp.where(l == mx, lane_f, float(LANES)), axis=1, keepdims=True)
        hot = lane_f == idx
        vals.append(mx)
        idxs.append(idx)
        hots.append(hot)
        member = member + jnp.where(hot, 1.0, 0.0)
        l = jnp.where(hot, -jnp.inf, l)
    ex = [jnp.exp(v - vals[0]) for v in vals]
    tot = sum(ex)
    stril = (lax.broadcasted_iota(I32, (tm, tm), 1) < lax.broadcasted_iota(I32, (tm, tm), 0))
    before = jnp.dot(jnp.where(stril, 1.0, 0.0).astype(BF16), member.astype(BF16),
                     preferred_element_type=F32) + carry_s[...]
    route = jnp.zeros((tm, LANES), F32)
    gates = jnp.zeros((tm, LANES), F32)
    for k in range(TOP_K_EXP):
        rank = jnp.sum(jnp.where(hots[k], before, 0.0), axis=1, keepdims=True)
        route = route + jnp.where(lane == k, idxs[k], 0.0) + jnp.where(lane == TOP_K_EXP + k, rank, 0.0)
        gates = gates + jnp.where(lane == k, ex[k] / tot, 0.0)
    route_ref[...] = route.astype(I32)
    gate_ref[...] = gates
    carry_s[...] = carry_s[...] + jnp.sum(member, axis=0, keepdims=True)
    cnt_ref[...] = carry_s[...]


def _scatter_rows_kernel(zstart_ref, dest_ref, h_ref, xs_out, zbuf, sem, zsem):
    ts = h_ref.shape[0]

    @pl.when(pl.program_id(0) == 0)
    def _():
        zbuf[...] = jnp.zeros_like(zbuf)

        def zero_block(j, go):
            @pl.when(zstart_ref[j] >= 0)
            def _():
                rows = pl.ds(pl.multiple_of(zstart_ref[j], ROW_BLOCK), ROW_BLOCK)
                go(pltpu.make_async_copy(zbuf, xs_out.at[rows], zsem))
        for j in range(2 * N_EXPERTS):
            zero_block(j, lambda cp: cp.start())
        for j in range(2 * N_EXPERTS):
            zero_block(j, lambda cp: cp.wait())

    def start(t, carry):
        for k in range(TOP_K_EXP):
            d = dest_ref[t * TOP_K_EXP + k]
            pltpu.make_async_copy(h_ref.at[pl.ds(t, 1)], xs_out.at[pl.ds(d, 1)], sem).start(priority=k % 2)
        return carry
    lax.fori_loop(0, ts, start, 0)

    def wait(t, carry):
        for k in range(TOP_K_EXP):
            pltpu.make_async_copy(h_ref.at[pl.ds(0, 1)], xs_out.at[pl.ds(0, 1)], sem).wait()
        return carry
    lax.fori_loop(0, ts, wait, 0)


def _expert_kernel(be_ref, na_ref, xs_ref, wgu_ref, bgu_ref, wdn_ref, bdn_ref, ys_ref):
    del be_ref
    active = pl.program_id(0) < na_ref[0]

    @pl.when(jnp.logical_not(active))
    def _():
        ys_ref[...] = jnp.zeros_like(ys_ref)

    @pl.when(active)
    def _():
        f = wdn_ref.shape[0]
        gu = jnp.dot(xs_ref[...].astype(BF16), wgu_ref[...], preferred_element_type=F32) + bgu_ref[...]
        gt = jnp.minimum(gu[:, :f], SWIGLU_LIMIT)
        up = jnp.clip(gu[:, f:], -SWIGLU_LIMIT, SWIGLU_LIMIT)
        act = (up + 1.0) * gt / (1.0 + jnp.exp(-SWIGLU_ALPHA * gt))
        ys_ref[...] = jnp.dot(act.astype(BF16), wdn_ref[...], preferred_element_type=F32) + bdn_ref[...]


def _combine_kernel(dest_ref, dest_next_ref, gate_ref, h_ref, g_ref, b_ref, ys_ref, o_ref, ybuf, sem):
    tc = h_ref.shape[0]
    i = pl.program_id(0)
    slot = i % 2

    def copy(s, t, k, d):
        return pltpu.make_async_copy(ys_ref.at[pl.ds(d, 1)], ybuf.at[s, k, pl.ds(t, 1)], sem.at[s])

    def start_tile(idx_ref, s):
        def start(t, carry):
            for k in range(TOP_K_EXP):
                copy(s, t, k, idx_ref[t * TOP_K_EXP + k]).start(priority=k % 2)
            return carry
        lax.fori_loop(0, tc, start, 0)

    @pl.when(i == 0)
    def _():
        start_tile(dest_ref, 0)

    @pl.when(i + 1 < pl.num_programs(0))
    def _():
        start_tile(dest_next_ref, 1 - slot)

    def wait(t, carry):
        for k in range(TOP_K_EXP):
            copy(slot, 0, k, 0).wait()
        return carry
    lax.fori_loop(0, tc, wait, 0)
    gates = gate_ref[...]
    y = sum(gates[:, k:k + 1] * ybuf[slot, k] for k in range(TOP_K_EXP))
    o_ref[...] = _layer_norm(DN_ALPHA * h_ref[...] + y, g_ref[...], b_ref[...])


def _finish(x2, m_out, a_out, ffn, tm, tr):
    n, d = x2.shape
    wo1, wo2, ln1_g, ln1_b, rw, rb, wgu, bgu, wdn, bdn, ln2_g, ln2_b = ffn
    f = wdn.shape[1]
    vec = lambda a: a.reshape(1, -1).astype(F32)
    row = lambda t, w: pl.BlockSpec((t, w), lambda i, *_: (i, 0))
    sds = jax.ShapeDtypeStruct
    h, route, gates, cnt = pl.pallas_call(
        _mix_router_kernel,
        grid=(n // tm,),
        in_specs=[row(tm, d), row(tm, W_M), row(tm, W_A), _const_spec(wo1.shape), _const_spec(wo2.shape),
                  _const_spec((1, d)), _const_spec((1, d)), _const_spec(rw.shape), _const_spec((1, LANES))],
        out_specs=[row(tm, d), row(tm, LANES), row(tm, LANES), pl.BlockSpec((1, LANES), lambda i: (0, 0))],
        out_shape=[sds((n, d), F32), sds((n, LANES), I32), sds((n, LANES), F32), sds((1, LANES), F32)],
        scratch_shapes=[pltpu.VMEM((1, LANES), F32)],
        compiler_params=_cparams(("arbitrary",)),
        name="mix_router",
    )(x2, m_out, a_out, wo1, wo2, vec(ln1_g), vec(ln1_b), rw, rb)

    counts = cnt[0, :N_EXPERTS].astype(I32)
    padded = (counts + ROW_BLOCK - 1) // ROW_BLOCK * ROW_BLOCK
    pad_end = jnp.cumsum(padded)
    pad_start = pad_end - padded
    dest = (pad_start[route[:, :TOP_K_EXP]] + route[:, TOP_K_EXP:2 * TOP_K_EXP]).reshape(-1)
    n_blk = -(-n * TOP_K_EXP // ROW_BLOCK) + N_EXPERTS
    blk_start = jnp.arange(n_blk, dtype=I32) * ROW_BLOCK
    blk_e = jnp.minimum(jnp.sum((pad_end[None, :] <= blk_start[:, None]).astype(I32), axis=1), N_EXPERTS - 1)
    n_act = (pad_end[-1:] // ROW_BLOCK).astype(I32)
    tail_rows = pad_end[-1] + jnp.arange(N_EXPERTS, dtype=I32) * ROW_BLOCK
    zero_blocks = jnp.concatenate([jnp.where(counts > 0, pad_end - ROW_BLOCK, -1),
                                   jnp.where(tail_rows < n_blk * ROW_BLOCK, tail_rows, -1)]).astype(I32)

    n_steps = n // tr
    smem_dest = pl.BlockSpec((tr * TOP_K_EXP,), lambda i, *_: (i,), memory_space=pltpu.SMEM)
    smem_next = pl.BlockSpec((tr * TOP_K_EXP,), lambda i, *_: (jnp.minimum(i + 1, n_steps - 1),),
                             memory_space=pltpu.SMEM)
    anyspec = pl.BlockSpec(memory_space=pl.ANY)
    xs = pl.pallas_call(
        _scatter_rows_kernel,
        grid_spec=pltpu.PrefetchScalarGridSpec(
            num_scalar_prefetch=1,
            grid=(n_steps,),
            in_specs=[smem_dest, row(tr, d)],
            out_specs=anyspec,
            scratch_shapes=[pltpu.VMEM((ROW_BLOCK, d), F32), pltpu.SemaphoreType.DMA(()),
                            pltpu.SemaphoreType.DMA(())],
        ),
        out_shape=sds((n_blk * ROW_BLOCK, d), F32),
        compiler_params=_cparams(("arbitrary",)),
        name="moe_scatter",
    )(zero_blocks, dest, h)

    blk = lambda i, be, na: jnp.minimum(i, na[0] - 1)
    ys = pl.pallas_call(
        _expert_kernel,
        grid_spec=pltpu.PrefetchScalarGridSpec(
            num_scalar_prefetch=2,
            grid=(n_blk,),
            in_specs=[pl.BlockSpec((ROW_BLOCK, d), lambda i, be, na: (blk(i, be, na), 0)),
                      pl.BlockSpec((None, d, 2 * f), lambda i, be, na: (be[i], 0, 0)),
                      pl.BlockSpec((None, 1, 2 * f), lambda i, be, na: (be[i], 0, 0)),
                      pl.BlockSpec((None, f, d), lambda i, be, na: (be[i], 0, 0)),
                      pl.BlockSpec((None, 1, d), lambda i, be, na: (be[i], 0, 0))],
            out_specs=pl.BlockSpec((ROW_BLOCK, d), lambda i, be, na: (i, 0)),
        ),
        out_shape=sds((n_blk * ROW_BLOCK, d), F32),
        compiler_params=_cparams(("arbitrary",)),
        name="moe_experts",
    )(blk_e, n_act, xs, wgu, bgu, wdn, bdn)

    return pl.pallas_call(
        _combine_kernel,
        grid=(n_steps,),
        in_specs=[smem_dest, smem_next, row(tr, LANES), row(tr, d), _const_spec((1, d)), _const_spec((1, d)),
                  anyspec],
        out_specs=row(tr, d),
        out_shape=sds((n, d), F32),
        scratch_shapes=[pltpu.VMEM((2, TOP_K_EXP, tr, d), F32), pltpu.SemaphoreType.DMA((2,))],
        compiler_params=_cparams(("arbitrary",)),
        name="moe_combine",
    )(dest, dest, gates, h, vec(ln2_g), vec(ln2_b), ys)


def _pack_ffn(w_out, ln1_g, ln1_b, router_w, router_b, w_gu, b_gu, w_dn, b_dn, ln2_g, ln2_b):
    d = w_out.shape[1]
    rw = jnp.pad(router_w, ((0, 0), (0, LANES - N_EXPERTS))).astype(BF16)
    rb = jnp.pad(router_b.astype(F32), (0, LANES - N_EXPERTS)).reshape(1, LANES)
    return (w_out[:W_M].astype(BF16), w_out[W_M:].astype(BF16), ln1_g, ln1_b, rw, rb,
            w_gu.astype(BF16), b_gu.astype(F32).reshape(N_EXPERTS, 1, -1),
            w_dn.astype(BF16), b_dn.astype(F32).reshape(N_EXPERTS, 1, d), ln2_g, ln2_b)


def _tile(n, pref):
    t = pref
    while n % t:
        t //= 2
    return t


def kernel(x_prompt, x_sample, cache_k, cache_v, cache_k_idx, state_C, state_n, state_m, page_table,
           w_in, b_ig, b_fg, mlstm_norm_g, w_out, ln1_g, ln1_b, router_w, router_b,
           w_gate_up, b_gate_up, w_down, b_down, ln2_g, ln2_b, rel_bias):
    assert w_in.shape[0] == DEPTH
    nb, seq, d = x_prompt.shape
    ns, steps, _ = x_sample.shape
    bias = _gate_bias(b_ig[0], b_fg[0])
    ffn = _pack_ffn(w_out[0], ln1_g[0], ln1_b[0], router_w[0], router_b[0], w_gate_up[0], b_gate_up[0],
                    w_down[0], b_down[0], ln2_g[0], ln2_b[0])

    xp = x_prompt.reshape(nb * seq, d)
    pp = _project(xp, w_in[0], nb, seq, _tile(seq, 256), True)
    mo_p, c_p, n_p, m_p = _mlstm_prompt(pp, bias, mlstm_norm_g[0], nb, seq, LANES)
    ao_p = _dsa_prompt(pp, rel_bias, nb, seq, LANES)
    y_p = _finish(xp, mo_p, ao_p, ffn, _tile(nb * seq, 512), _tile(nb * seq, 256))

    xs = x_sample.reshape(ns * steps, d)
    ps = _project(xs, w_in[0], 1, ns * steps, _tile(ns * steps, 256), False)
    mo_s, c_s, n_s, m_s = _mlstm_sample(ps, bias, mlstm_norm_g[0], state_C[0], state_n[0], state_m[0], ns, steps)
    ao_s = _dsa_sample(ps, cache_k[0], cache_v[0], cache_k_idx[0], page_table, rel_bias, ns, steps)
    y_s = _finish(xs, mo_s, ao_s, ffn, _tile(ns * steps, 512), _tile(ns * steps, 256))

    dt = x_prompt.dtype
    L = lambda a: a[None]
    heads = lambda a: jnp.transpose(a.reshape(nb, H_A, D_HA, seq), (0, 3, 1, 2))
    return (y_p.reshape(nb, seq, d), y_s.reshape(ns, steps, d),
            L(heads(pp["ktf"])), L(heads(pp["vtf"])), L(jnp.transpose(pp["iktf"], (0, 2, 1))),
            L(c_p.astype(dt)), L(n_p.astype(dt)), L(m_p.astype(dt)),
            L(ps["ak"].reshape(ns, steps, H_A, D_HA)), L(ps["av"].reshape(ns, steps, H_A, D_HA)),
            L(ps["ik"].reshape(ns, steps, D_I)),
            L(c_s.astype(state_C.dtype)), L(n_s.astype(state_n.dtype)), L(m_s.astype(state_m.dtype)))
```

```python
import functools
import math

import numpy as np
import jax
import jax.numpy as jnp
from jax import lax
from jax.experimental import pallas as pl
from jax.experimental.pallas import tpu as pltpu

F32, BF16, I32 = jnp.float32, jnp.bfloat16, jnp.int32

H_M, D_HM = 4, 128
H_A, D_HA = 8, 64
H_I, D_I = 4, 64
W_M, W_A = H_M * D_HM, H_A * D_HA
TOPK_MAX = 256
N_BUCKETS, MAX_DISTANCE = 32, 128
N_EXPERTS, TOP_K_EXP = 32, 4
SWIGLU_LIMIT, SWIGLU_ALPHA = 7.0, 1.702
GATE_CAP = 15.0
LN_EPS = 1e-5
DEPTH = 1
DN_ALPHA = (2.0 * DEPTH) ** 0.25

LANES = 128
VMEM_LIMIT_BYTES = 56 * 1024 * 1024
LOG2E = 1.4426950408889634
NEG = -1e30
NEG_BITS = int(np.float32(NEG).view(np.int32))
INT_MIN = -2 ** 31
KEY_POS_INF = 0x7F800000
KEY_NEG_INF = (0xFF800000 ^ 0x7FFFFFFF) - 2 ** 32
KEY_CHUNK = 512
TAIL = KEY_CHUNK + LANES
ROW_BLOCK = 512
BRACKET_ITERS = 24
KV_SLOTS = 4
SMALL_IK, SMALL_MI, SMALL_MF, SMALL_IW = 0, 64, 68, 72

_NT = (((1,), (1,)), ((), ()))


def _cparams(sem):
    return pltpu.CompilerParams(dimension_semantics=sem, vmem_limit_bytes=VMEM_LIMIT_BYTES)


def _const_spec(shape):
    zeros = (0,) * len(shape)
    return pl.BlockSpec(shape, lambda *_: zeros, pipeline_mode=pl.Buffered(1))


_PROJ_COMMON = ("mq", "mk", "mv", "mo", "sm", "aq", "qs", "lo", "hi")
_PROJ_EXTRA = {
    True: ("avb", "ktb", "iktb", "ktf", "vtf", "iktf"),
    False: ("ak", "av", "ik"),
}


def _proj_kernel(x_ref, wm_ref, wa_ref, wi_ref, ws_ref, wt_ref, *out_refs, transposed):
    o = dict(zip(_PROJ_COMMON + _PROJ_EXTRA[transposed], out_refs))
    xb = x_ref[...].astype(BF16)
    ym = jnp.dot(xb, wm_ref[...], preferred_element_type=F32)
    o["mq"][...] = ym[:, 0:W_M].astype(BF16)
    o["mk"][...] = (ym[:, W_M:2 * W_M] * (D_HM ** -0.5)).astype(BF16)
    o["mv"][...] = ym[:, 2 * W_M:3 * W_M].astype(BF16)
    o["mo"][...] = ym[:, 3 * W_M:4 * W_M]
    ya = jnp.dot(xb, wa_ref[...], preferred_element_type=F32)
    o["aq"][...] = (ya[:, 0:W_A] * (D_HA ** -0.5 * (LOG2E if transposed else 1.0))).astype(BF16)
    xl = (x_ref[...] - xb.astype(F32)).astype(BF16)
    ys = (jnp.dot(xb, ws_ref[0], preferred_element_type=F32) + jnp.dot(xl, ws_ref[0], preferred_element_type=F32)
          + jnp.dot(xb, ws_ref[1], preferred_element_type=F32))
    o["sm"][...] = ys
    yi = jnp.dot(xb, wi_ref[...], preferred_element_type=F32)
    wsc = ys * (H_I ** -0.5 * D_I ** -0.5)
    lane = lax.broadcasted_iota(I32, yi.shape, 1)
    scale = wsc[:, SMALL_IW + 3:SMALL_IW + 4]
    for h in (2, 1, 0):
        scale = jnp.where(lane < (h + 1) * D_I, wsc[:, SMALL_IW + h:SMALL_IW + h + 1], scale)
    o["qs"][...] = (yi * scale).astype(BF16)
    o["lo"][...] = jnp.where(ys >= 0.0, 0.0, -jnp.inf)
    o["hi"][...] = jnp.where(ys <= 0.0, 0.0, jnp.inf)
    if transposed:
        o["avb"][...] = ya[:, W_A:2 * W_A].astype(BF16)
        yt = lax.dot_general(wt_ref[...], xb, _NT, preferred_element_type=F32)
        o["ktf"][...] = yt[0:W_A]
        o["ktb"][...] = yt[0:W_A].astype(BF16)
        o["vtf"][...] = yt[W_A:2 * W_A]
        o["iktf"][...] = yt[2 * W_A:2 * W_A + D_I]
        o["iktb"][...] = yt[2 * W_A:2 * W_A + D_I].astype(BF16)
    else:
        o["ak"][...] = ya[:, W_A:2 * W_A]
        o["av"][...] = ya[:, 2 * W_A:3 * W_A]
        o["ik"][...] = ys[:, SMALL_IK:SMALL_IK + D_I]


def _pack_w_in(w_in, transposed):
    d = w_in.shape[0]
    c = np.cumsum([0, W_M, W_M, W_M, W_M, H_M, H_M, W_A, W_A, W_A, H_I * D_I, D_I, H_I])
    seg = lambda i: w_in[:, c[i]:c[i + 1]]
    wm = w_in[:, c[0]:c[4]]
    wa = jnp.concatenate([seg(6), seg(8)], axis=1) if transposed else w_in[:, c[6]:c[9]]
    wi = seg(9)
    ws = jnp.concatenate([seg(10), seg(4), seg(5), seg(11),
                          jnp.zeros((d, LANES - D_I - 2 * H_M - H_I), w_in.dtype)], axis=1)
    wt = jnp.concatenate([seg(7), seg(8), seg(10)], axis=1).T
    ws_hi = ws.astype(BF16)
    ws2 = jnp.stack([ws_hi, (ws - ws_hi.astype(F32)).astype(BF16)])
    return (wm.astype(BF16), wa.astype(BF16), wi.astype(BF16), ws2, wt.astype(BF16))


def _project(x2, w_in, nb, seq, tm, transposed):
    n, d = x2.shape
    packs = _pack_w_in(w_in, transposed)
    nps = seq // tm
    sds = jax.ShapeDtypeStruct
    row = lambda w, dt: (sds((n, w), dt), pl.BlockSpec((tm, w), lambda i: (i, 0)))
    tr = lambda w, dt: (sds((nb, w, seq), dt), pl.BlockSpec((None, w, tm), lambda i: (i // nps, 0, i % nps)))
    outs = dict(
        mq=row(W_M, BF16), mk=row(W_M, BF16), mv=row(W_M, BF16), mo=row(W_M, F32), sm=row(LANES, F32),
        aq=row(W_A, BF16), qs=row(H_I * D_I, BF16), lo=row(LANES, F32), hi=row(LANES, F32),
        avb=row(W_A, BF16), ktb=tr(W_A, BF16), iktb=tr(D_I, BF16), ktf=tr(W_A, F32), vtf=tr(W_A, F32),
        iktf=tr(D_I, F32), ak=row(W_A, F32), av=row(W_A, F32), ik=row(D_I, F32),
    )
    names = _PROJ_COMMON + _PROJ_EXTRA[transposed]
    res = pl.pallas_call(
        functools.partial(_proj_kernel, transposed=transposed),
        grid=(n // tm,),
        in_specs=[pl.BlockSpec((tm, d), lambda i: (i, 0))] + [_const_spec(w.shape) for w in packs],
        out_specs=[outs[k][1] for k in names],
        out_shape=[outs[k][0] for k in names],
        compiler_params=_cparams(("arbitrary",)),
        name="proj_t" if transposed else "proj",
    )(x2, *packs)
    return dict(zip(names, res))


def _gates(sm, bias):
    cap = GATE_CAP * jnp.tanh((sm + bias) / GATE_CAP)
    e = jnp.exp(-jnp.abs(cap))
    u = 1.0 + e
    log1p_e = jnp.where(u == 1.0, e, jnp.log(u) * (e / (u - 1.0)))
    logsig = jnp.minimum(cap, 0.0) - log1p_e
    lane = lax.broadcasted_iota(I32, sm.shape, 1)
    return jnp.where(lane >= SMALL_MF, logsig, cap)


def _split3(x):
    a = x.astype(BF16)
    r = x - a.astype(F32)
    b = r.astype(BF16)
    c = (r - b.astype(F32)).astype(BF16)
    return a, b, c


def _dot3(mat, parts):
    return sum(jnp.dot(mat, p, preferred_element_type=F32) for p in parts)


def _headnorm_out(hh, mo, g):
    mu = jnp.mean(hh, axis=1, keepdims=True)
    var = jnp.mean(jnp.square(hh - mu), axis=1, keepdims=True)
    hn = (hh - mu) * lax.rsqrt(var + LN_EPS) * g
    return (hn / (1.0 + jnp.exp(-mo))).astype(BF16)


def _mlstm_prompt_kernel(q_ref, k_ref, v_ref, mo_ref, sm_ref, bias_ref, g_ref,
                         out_ref, c_out, n_out, m_out, c_s, n_s, m_s, *, chunk):
    j = pl.program_id(1)

    @pl.when(j == 0)
    def _():
        c_s[...] = jnp.zeros_like(c_s)
        n_s[...] = jnp.zeros_like(n_s)
        m_s[...] = jnp.zeros_like(m_s)

    L = chunk
    gate = _gates(sm_ref[...], bias_ref[...])
    r_i = lax.broadcasted_iota(I32, (L, L), 0)
    c_i = lax.broadcasted_iota(I32, (L, L), 1)
    tril = c_i <= r_i
    cum = _dot3(jnp.where(tril, 1.0, 0.0).astype(BF16), _split3(gate))
    gate_t = gate.T
    cum_t = cum.T
    for h in range(H_M):
        sl = slice(h * D_HM, (h + 1) * D_HM)
        b_col = cum[:, SMALL_MF + h:SMALL_MF + h + 1]
        b_row = cum_t[SMALL_MF + h:SMALL_MF + h + 1, :]
        i_col = gate[:, SMALL_MI + h:SMALL_MI + h + 1]
        i_row = gate_t[SMALL_MI + h:SMALL_MI + h + 1, :]
        m_prev = m_s[h][:, 0:1]
        dmat = jnp.where(tril, b_col - b_row + i_row, -jnp.inf)
        inter = b_col + m_prev
        mt = jnp.maximum(inter, jnp.max(dmat, axis=1, keepdims=True))
        w_intra = jnp.exp(dmat - mt)
        w_inter = jnp.exp(inter - mt)
        q, k, v = q_ref[:, sl], k_ref[:, sl], v_ref[:, sl]
        s = lax.dot_general(q, k, _NT, preferred_element_type=F32) * w_intra
        c_old = c_s[h]
        n_old = n_s[h]
        num = (jnp.dot(s.astype(BF16), v, preferred_element_type=F32)
               + w_inter * lax.dot_general(q, c_old.astype(BF16), _NT, preferred_element_type=F32))
        den = (jnp.sum(s, axis=1, keepdims=True)
               + w_inter * jnp.sum(q.astype(F32) * n_old, axis=1, keepdims=True))
        hh = num / jnp.maximum(jnp.abs(den), jnp.exp(-mt))
        out_ref[:, sl] = _headnorm_out(hh, mo_ref[:, sl], g_ref[:, sl])
        bl = b_row[:, L - 1:L]
        decay = bl - b_col + i_col
        m_new = jnp.maximum(bl + m_prev, jnp.max(decay, axis=0, keepdims=True))
        ws = jnp.exp(decay - m_new)
        cs = jnp.exp(bl + m_prev - m_new)
        vw_t = (v.astype(F32) * ws).T.astype(BF16)
        c_s[h] = cs * c_old + jnp.dot(vw_t, k, preferred_element_type=F32)
        n_s[h] = cs * n_old + jnp.sum(k.astype(F32) * ws, axis=0, keepdims=True)
        m_s[h] = jnp.broadcast_to(m_new, (1, LANES))

    @pl.when(j == pl.num_programs(1) - 1)
    def _():
        c_out[...] = c_s[...]
        n_out[...] = n_s[...]
        m_out[...] = m_s[...]


def _gate_bias(b_ig, b_fg):
    z = jnp.zeros((LANES,), F32)
    z = z.at[SMALL_MI:SMALL_MI + H_M].set(b_ig.astype(F32)).at[SMALL_MF:SMALL_MF + H_M].set(b_fg.astype(F32))
    return z.reshape(1, LANES)


def _mlstm_prompt(pr, bias, norm_g, nb, seq, chunk):
    n = nb * seq
    nc = seq // chunk
    row = lambda w: pl.BlockSpec((chunk, w), lambda b, j: (b * nc + j, 0))
    st = lambda *s: pl.BlockSpec((None,) + s, lambda b, j: (b,) + (0,) * len(s))
    sds = jax.ShapeDtypeStruct
    out, c, nn, m = pl.pallas_call(
        functools.partial(_mlstm_prompt_kernel, chunk=chunk),
        grid=(nb, nc),
        in_specs=[row(W_M), row(W_M), row(W_M), row(W_M), row(LANES),
                  pl.BlockSpec((1, LANES), lambda b, j: (0, 0)), pl.BlockSpec((1, W_M), lambda b, j: (0, 0))],
        out_specs=[row(W_M), st(H_M, D_HM, D_HM), st(H_M, 1, D_HM), st(H_M, 1, LANES)],
        out_shape=[sds((n, W_M), BF16), sds((nb, H_M, D_HM, D_HM), F32),
                   sds((nb, H_M, 1, D_HM), F32), sds((nb, H_M, 1, LANES), F32)],
        scratch_shapes=[pltpu.VMEM((H_M, D_HM, D_HM), F32), pltpu.VMEM((H_M, 1, D_HM), F32),
                        pltpu.VMEM((H_M, 1, LANES), F32)],
        compiler_params=_cparams(("arbitrary", "arbitrary")),
        name="mlstm_prompt",
    )(pr["mq"], pr["mk"], pr["mv"], pr["mo"], pr["sm"], bias, norm_g.reshape(1, W_M).astype(F32))
    return out, c, nn.reshape(nb, H_M, D_HM), m[:, :, 0, 0]


def _mlstm_sample_kernel(q_ref, k_ref, v_ref, mo_ref, sm_ref, bias_ref, g_ref, mrow_ref, nrow_ref,
                         c_in, n_in, out_ref, c_out, n_out, m_out, numi_s, *, steps):
    h = pl.program_id(1)
    R = q_ref.shape[0]
    nseq = R // steps
    gate = _gates(sm_ref[...], bias_ref[...])
    r_i = lax.broadcasted_iota(I32, (R, R), 0)
    c_i = lax.broadcasted_iota(I32, (R, R), 1)
    same = (r_i // steps) == (c_i // steps)
    mask = jnp.logical_and(same, c_i <= r_i)
    parts = _split3(gate)
    cum = _dot3(jnp.where(mask, 1.0, 0.0).astype(BF16), parts)
    tot = _dot3(jnp.where(same, 1.0, 0.0).astype(BF16), parts)
    lane = lax.broadcasted_iota(I32, (R, LANES), 1)
    sub = lax.broadcasted_iota(I32, (LANES, R), 0)
    col = lambda a, off: jnp.sum(jnp.where(lane == off + h, a, 0.0), axis=1, keepdims=True)
    rowv = lambda a, off: jnp.sum(jnp.where(sub == off + h, a.T, 0.0), axis=0, keepdims=True)
    b_col, b_row = col(cum, SMALL_MF), rowv(cum, SMALL_MF)
    i_col, i_row = col(gate, SMALL_MI), rowv(gate, SMALL_MI)
    bl_col, bl_row = col(tot, SMALL_MF), rowv(tot, SMALL_MF)
    m_col = jnp.sum(jnp.where(lane == h, mrow_ref[...], 0.0), axis=1, keepdims=True)
    dmat = jnp.where(mask, b_col - b_row + i_row, -jnp.inf)
    inter = b_col + m_col
    mt = jnp.maximum(inter, jnp.max(dmat, axis=1, keepdims=True))
    w_intra = jnp.exp(dmat - mt)
    w_inter = jnp.exp(inter - mt)
    q, k, v = q_ref[...], k_ref[...], v_ref[...]
    s = lax.dot_general(q, k, _NT, preferred_element_type=F32) * w_intra
    for b in range(nseq):
        rows = slice(b * steps, (b + 1) * steps)
        numi_s[rows, :] = lax.dot_general(q[rows, :], c_in[b].astype(BF16), _NT, preferred_element_type=F32)
    num = jnp.dot(s.astype(BF16), v, preferred_element_type=F32) + w_inter * numi_s[...]
    den = (jnp.sum(s, axis=1, keepdims=True)
           + w_inter * jnp.sum(q.astype(F32) * nrow_ref[...], axis=1, keepdims=True))
    hh = num / jnp.maximum(jnp.abs(den), jnp.exp(-mt))
    out_ref[...] = _headnorm_out(hh, mo_ref[...], g_ref[...])
    decay_col = bl_col - b_col + i_col
    decay_row = bl_row - b_row + i_row
    seg_max = jnp.max(jnp.where(same, decay_row, -jnp.inf), axis=1, keepdims=True)
    m_new = jnp.maximum(bl_col + m_col, seg_max)
    ws = jnp.exp(decay_col - m_new)
    cs = jnp.exp(bl_col + m_col - m_new)
    kf = k.astype(F32)
    vw_t = (v.astype(F32) * ws).T.astype(BF16)
    kw = kf * ws
    rid = lax.broadcasted_iota(I32, (R, 1), 0) // steps
    for b in range(nseq):
        r0 = b * steps
        kb = jnp.where(rid == b, kf, 0.0).astype(BF16)
        cs_b = cs[r0:r0 + 1, :]
        c_out[b] = cs_b * c_in[b] + jnp.dot(vw_t, kb, preferred_element_type=F32)
        n_out[b] = cs_b * n_in[b] + jnp.sum(kw[r0:r0 + steps, :], axis=0, keepdims=True)
        m_out[b] = jnp.broadcast_to(m_new[r0:r0 + 1, :], (1, LANES))


def _mlstm_sample(pr, bias, norm_g, state_c, state_n, state_m, nseq_total, steps):
    n = nseq_total * steps
    R = LANES
    spt = R // steps
    m_rows = jnp.pad(jnp.repeat(state_m.astype(F32), steps, axis=0), ((0, 0), (0, LANES - H_M)))
    n_rows = jnp.repeat(state_n.astype(F32).reshape(nseq_total, W_M), steps, axis=0)
    n4 = state_n.astype(F32).reshape(nseq_total, H_M, 1, D_HM)
    colh = lambda w: pl.BlockSpec((R, w), lambda i, h: (i, h))
    sth = lambda *s: pl.BlockSpec((spt, None) + s, lambda i, h: (i, h) + (0,) * len(s))
    sds = jax.ShapeDtypeStruct
    out, c, nn, m = pl.pallas_call(
        functools.partial(_mlstm_sample_kernel, steps=steps),
        grid=(n // R, H_M),
        in_specs=[colh(D_HM), colh(D_HM), colh(D_HM), colh(D_HM),
                  pl.BlockSpec((R, LANES), lambda i, h: (i, 0)),
                  pl.BlockSpec((1, LANES), lambda i, h: (0, 0)),
                  pl.BlockSpec((1, D_HM), lambda i, h: (0, h)),
                  pl.BlockSpec((R, LANES), lambda i, h: (i, 0)),
                  colh(D_HM), sth(D_HM, D_HM), sth(1, D_HM)],
        out_specs=[colh(D_HM), sth(D_HM, D_HM), sth(1, D_HM), sth(1, LANES)],
        out_shape=[sds((n, W_M), BF16), sds((nseq_total, H_M, D_HM, D_HM), F32),
                   sds((nseq_total, H_M, 1, D_HM), F32), sds((nseq_total, H_M, 1, LANES), F32)],
        scratch_shapes=[pltpu.VMEM((R, D_HM), F32)],
        compiler_params=_cparams(("arbitrary", "arbitrary")),
        name="mlstm_sample",
    )(pr["mq"], pr["mk"], pr["mv"], pr["mo"], pr["sm"], bias, norm_g.reshape(1, W_M).astype(F32),
      m_rows, n_rows, state_c.astype(F32), n4)
    return out, c, nn.reshape(nseq_total, H_M, D_HM), m[:, :, 0, 0]


def _rel_bucket(dist):
    max_exact = N_BUCKETS // 2
    d = jnp.maximum(dist, 0)
    df = jnp.maximum(d.astype(F32), 1.0)
    large = max_exact + (jnp.log(df / max_exact) / math.log(MAX_DISTANCE / max_exact)
                         * (N_BUCKETS - max_exact)).astype(I32)
    large = jnp.minimum(large, N_BUCKETS - 1)
    return jnp.where(d < max_exact, d, large)


def _bias_delta(rel_bias, dist):
    far = rel_bias[_rel_bucket(jnp.asarray(4 * MAX_DISTANCE, I32))]
    return (rel_bias[_rel_bucket(dist)] - far).astype(F32)


def _sortable(score):
    score = jnp.where(score == 0.0, 0.0, score)
    bits = pltpu.bitcast(score, I32)
    return bits ^ ((bits >> 31) & 0x7FFFFFFF)


def _key_value(key):
    return pltpu.bitcast(key ^ ((key >> 31) & 0x7FFFFFFF), F32)


def _bracket_search(count_ge, lo_k, c_lo, hi_k, c_hi, active, topk, iters):
    logk = math.log(topk)
    fval = lambda c: jnp.log(c + 0.5) - logk

    def cond(st):
        return jnp.logical_and(st[0] < iters, jnp.max(st[-1]) > 0.5)

    def body(st):
        it, lo_k, hi_k, f_lo, f_hi, c_lo, side, act = st
        lo_v, hi_v = _key_value(lo_k), _key_value(hi_k)
        w = jnp.clip(f_lo / (f_lo - f_hi), 1.0 / 64, 63.0 / 64)
        t_k = _sortable(lo_v + (hi_v - lo_v) * w)
        mid = (lo_k >> 1) + (hi_k >> 1) + (lo_k & hi_k & 1)
        t_k = jnp.where(jnp.logical_or(t_k <= lo_k, t_k >= hi_k), mid, t_k)
        c = count_ge(t_k)
        live = act > 0.5
        up_lo = jnp.logical_and(live, c >= topk)
        up_hi = jnp.logical_and(live, c < topk)
        fn = fval(c)
        f_hi = jnp.where(jnp.logical_and(up_lo, side > 0.5), 0.5 * f_hi, f_hi)
        f_lo = jnp.where(jnp.logical_and(up_hi, side < -0.5), 0.5 * f_lo, f_lo)
        lo_k = jnp.where(up_lo, t_k, lo_k)
        f_lo = jnp.where(up_lo, fn, f_lo)
        c_lo = jnp.where(up_lo, c, c_lo)
        hi_k = jnp.where(up_hi, t_k, hi_k)
        f_hi = jnp.where(up_hi, fn, f_hi)
        side = jnp.where(up_lo, 1.0, jnp.where(up_hi, -1.0, side))
        open_ = jnp.logical_and(c_lo != topk, hi_k > lo_k + 1)
        act = jnp.where(jnp.logical_and(live, open_), 1.0, 0.0)
        return it + 1, lo_k, hi_k, f_lo, f_hi, c_lo, side, act

    open0 = jnp.logical_and(c_lo != topk, hi_k > lo_k + 1)
    act0 = jnp.where(jnp.logical_and(active, open0), 1.0, 0.0)
    init = (jnp.int32(0), lo_k, hi_k, fval(c_lo), fval(c_hi), c_lo, jnp.zeros_like(c_lo), act0)
    st = lax.while_loop(cond, body, init)
    return st[1], st[5], st[-1] > 0.5


def _kth_largest_search(count_ge, total, short, topk, rows, kmin=None, kmax=None):
    def cond(st):
        bit, _, _, done = st
        return jnp.logical_and(bit >= 0, jnp.min(done) < 0.5)

    def body(st):
        bit, pref, cnt, done = st
        cand_u = pref | lax.shift_left(jnp.int32(1), bit)
        c = count_ge(cand_u ^ INT_MIN)
        take = jnp.logical_and(c >= topk, done < 0.5)
        pref = jnp.where(take, cand_u, pref)
        cnt = jnp.where(take, c, cnt)
        done = jnp.where(cnt == topk, 1.0, done)
        return bit - 1, pref, cnt, done

    c_nonneg = count_ge(jnp.zeros((rows, 1), I32))
    c_pos = count_ge(jnp.ones((rows, 1), I32))
    nonneg = c_nonneg >= topk
    zero_thr = jnp.logical_and(nonneg, c_pos < topk)
    pref0 = jnp.where(nonneg, INT_MIN, 0)
    cnt0 = jnp.where(nonneg, c_nonneg, total)
    done0 = jnp.logical_or(jnp.logical_or(short, zero_thr), cnt0 == topk)
    if kmin is not None:
        lo_k = jnp.where(nonneg, 1, kmin)
        hi_k = jnp.where(nonneg, kmax + 1, 0)
        c_lo = jnp.where(nonneg, c_pos, total)
        c_hi = jnp.where(nonneg, 0.0, c_nonneg)
        lo_k, c_lo, left_open = _bracket_search(count_ge, lo_k, c_lo, hi_k, c_hi, jnp.logical_not(done0),
                                                topk, BRACKET_ITERS)
        found = jnp.logical_and(jnp.logical_not(done0), jnp.logical_not(left_open))
        done0 = jnp.logical_or(done0, found)
    init = (jnp.int32(30), pref0, cnt0, jnp.where(done0, 1.0, 0.0))
    _, pref, cnt, _ = lax.while_loop(cond, body, init)
    thr = pref ^ INT_MIN
    if kmin is not None:
        thr = jnp.where(found, lo_k, thr)
        cnt = jnp.where(found, c_lo, cnt)
    thr = jnp.where(short, INT_MIN + 1, thr)
    return thr, jnp.logical_and(jnp.logical_not(short), cnt != topk)


def _dsa_prompt_kernel(qs_ref, lo_ref, hi_ref, aq_ref, ikt_ref, kt_ref, v_ref, dz_ref, dzmax_ref, o_ref,
                       buf_ref, q2_ref, bnd_ref, kmax_ref, m_ref, l_ref, ls2_ref, acc2_ref, *, tq, topk):
    CH = KEY_CHUNK
    q0 = pl.program_id(1) * tq
    nchunk = jnp.maximum(lax.shift_right_logical(q0 + tq + 2 * CH - 1, 10), 1) * 2

    @pl.when(pl.program_id(1) == 0)
    def _():
        def body(c, mx):
            kc = kt_ref[:, pl.ds(pl.multiple_of(c * CH, CH), CH)].astype(F32)
            sq = kc * kc
            parts = [jnp.sum(sq[h * D_HA:(h + 1) * D_HA], axis=0, keepdims=True) for h in range(H_A)]
            return jnp.maximum(mx, jnp.concatenate(parts, axis=0))
        mx = lax.fori_loop(0, kt_ref.shape[1] // CH, body, jnp.zeros((H_A, CH), F32))
        km = jnp.sqrt(jnp.max(mx, axis=1, keepdims=True))
        for h in range(H_A):
            kmax_ref[h] = jnp.broadcast_to(km[h:h + 1, :], (1, LANES))

    qs = qs_ref[...]
    lo = lo_ref[...]
    hi = hi_ref[...]
    qs_hm = jnp.concatenate([qs[:, h * D_I:(h + 1) * D_I] for h in range(H_I)], axis=0)
    lo_hm = jnp.concatenate([lo[:, SMALL_IW + h:SMALL_IW + h + 1] for h in range(H_I)], axis=0)
    hi_hm = jnp.concatenate([hi[:, SMALL_IW + h:SMALL_IW + h + 1] for h in range(H_I)], axis=0)
    qpos = q0 + lax.broadcasted_iota(I32, (tq, CH), 0)
    lane_ch = lax.broadcasted_iota(I32, (tq, CH), 1)

    def fold(op, x):
        return functools.reduce(op, [x[:, g * LANES:(g + 1) * LANES] for g in range(x.shape[1] // LANES)])

    def score_chunk(c, carry):
        kmax_p, kmin_p = carry
        st = pl.multiple_of(c * CH, CH)
        a = jnp.dot(qs_hm, ikt_ref[:, pl.ds(st, CH)], preferred_element_type=F32)
        t = jnp.maximum(jnp.minimum(a, hi_hm), lo_hm)
        sc = sum(t[h * tq:(h + 1) * tq] for h in range(H_I))
        valid = st + lane_ch <= qpos
        key = _sortable(sc)
        buf_ref[:, pl.ds(st, CH)] = jnp.where(valid, key, INT_MIN)
        return (jnp.maximum(kmax_p, fold(jnp.maximum, jnp.where(valid, key, KEY_NEG_INF))),
                jnp.minimum(kmin_p, fold(jnp.minimum, jnp.where(valid, key, KEY_POS_INF))))

    kmax_p, kmin_p = lax.fori_loop(0, nchunk, score_chunk, (jnp.full((tq, LANES), KEY_NEG_INF, I32),
                                                           jnp.full((tq, LANES), KEY_POS_INF, I32)))
    kmax = _sortable(jnp.max(_key_value(kmax_p), axis=1, keepdims=True))
    kmin = _sortable(jnp.min(_key_value(kmin_p), axis=1, keepdims=True))

    def count_ge(cand):
        candb = jnp.broadcast_to(cand, (tq, LANES))

        def body(c, acc):
            st = c * (2 * CH)
            for g in range(2 * CH // LANES):
                blk = buf_ref[:, pl.ds(pl.multiple_of(st + g * LANES, LANES), LANES)]
                acc = acc + jnp.where(blk >= candb, 1.0, 0.0)
            return acc

        acc = lax.fori_loop(0, nchunk // 2, body, jnp.zeros((tq, LANES), F32))
        return jnp.sum(acc, axis=1, keepdims=True)

    nvalid = q0 + 1 + lax.broadcasted_iota(I32, (tq, 1), 0)
    short = nvalid <= topk
    thr, tie = _kth_largest_search(count_ge, (nchunk * CH).astype(F32), short, topk, tq, kmin, kmax)
    any_tie = jnp.max(jnp.where(tie, 1.0, 0.0)) > 0.5

    @pl.when(jnp.logical_not(any_tie))
    def _():
        def fin(c, carry):
            st = pl.multiple_of(c * (2 * CH), 2 * CH)
            blk = buf_ref[:, pl.ds(st, 2 * CH)]
            buf_ref[:, pl.ds(st, 2 * CH)] = jnp.where(blk >= thr, 0, NEG_BITS)
            return carry
        lax.fori_loop(0, nchunk // 2, fin, 0)

    @pl.when(any_tie)
    def _():
        need = topk - count_ge(thr + 1)
        upper = (lax.broadcasted_iota(I32, (CH, CH), 0) < lax.broadcasted_iota(I32, (CH, CH), 1))
        upper = jnp.where(upper, 1.0, 0.0).astype(BF16)

        def fin(c, seen):
            st = pl.multiple_of(c * CH, CH)
            blk = buf_ref[:, pl.ds(st, CH)]
            eq = jnp.where(blk == thr, 1.0, 0.0)
            before = jnp.dot(eq.astype(BF16), upper, preferred_element_type=F32) + seen
            sel = jnp.logical_or(blk > thr, jnp.logical_and(blk == thr, before < need))
            buf_ref[:, pl.ds(st, CH)] = jnp.where(sel, 0, NEG_BITS)
            return seen + jnp.sum(eq, axis=1, keepdims=True)
        lax.fori_loop(0, nchunk, fin, jnp.zeros((tq, 1), F32))

    lane = lax.broadcasted_iota(I32, (tq, LANES), 1)
    left = lane < D_HA
    for hp in range(H_A // 2):
        aqp = aq_ref[:, hp * LANES:(hp + 1) * LANES]
        zero = jnp.zeros_like(aqp)
        q2_ref[hp, 0:tq, :] = jnp.where(left, aqp, zero)
        q2_ref[hp, tq:2 * tq, :] = jnp.where(left, zero, aqp)
    ts = jnp.maximum(q0 - CH, 0)
    nfar = lax.shift_right_logical(ts + CH - 1, 9)
    lim = nfar * CH - ts
    covered = jnp.where(lax.broadcasted_iota(I32, (tq, TAIL), 1) < lim, NEG, 0.0)
    dz_start = pl.multiple_of(CH - (q0 - ts), LANES)
    two = lambda a: jnp.concatenate([a, a], axis=0)

    def logits(hp, st, width, mb2, with_dz):
        kt = kt_ref[hp * LANES:(hp + 1) * LANES, pl.ds(st, width)]
        s = jnp.dot(q2_ref[hp], kt, preferred_element_type=F32) + mb2
        if with_dz:
            s = s + jnp.concatenate([dz_ref[2 * hp, :, pl.ds(dz_start, width)],
                                     dz_ref[2 * hp + 1, :, pl.ds(dz_start, width)]], axis=0)
        return s

    def mask_bias(st, width, extra):
        mb = pltpu.bitcast(buf_ref[:, pl.ds(st, width)], F32)
        return two(mb if extra is None else mb + extra)

    def lane_groups(p):
        return sum(p[:, g * LANES:(g + 1) * LANES] for g in range(p.shape[1] // LANES))

    for hp in range(H_A // 2):
        q2 = q2_ref[hp].astype(F32)
        qn = jnp.sqrt(jnp.sum(q2 * q2, axis=1, keepdims=True))
        r2 = lax.broadcasted_iota(I32, (2 * tq, 1), 0)
        kmax = jnp.where(r2 < tq, kmax_ref[2 * hp][:, 0:1], kmax_ref[2 * hp + 1][:, 0:1])
        dzmax = jnp.where(r2 < tq, dzmax_ref[2 * hp][:, 0:1], dzmax_ref[2 * hp + 1][:, 0:1])
        bnd_ref[hp] = qn * kmax + dzmax
    acc2_ref[...] = jnp.zeros(acc2_ref.shape, F32)
    ls2_ref[...] = jnp.zeros(ls2_ref.shape, F32)

    def attend_fast(st, width, extra, with_dz):
        mb2 = mask_bias(st, width, extra)
        for hp in range(H_A // 2):
            p = jnp.exp2(logits(hp, st, width, mb2, with_dz) - bnd_ref[hp])
            ls2_ref[hp] = ls2_ref[hp] + lane_groups(p)
            vv = v_ref[pl.ds(st, width), hp * LANES:(hp + 1) * LANES]
            acc2_ref[hp] = acc2_ref[hp] + jnp.dot(p.astype(BF16), vv, preferred_element_type=F32)

    def far_fast(jc, carry):
        attend_fast(pl.multiple_of(jc * 2 * CH, 2 * CH), 2 * CH, None, False)
        return carry

    lax.fori_loop(0, nfar // 2, far_fast, 0)

    @pl.when(nfar % 2 == 1)
    def _():
        attend_fast(pl.multiple_of((nfar - 1) * CH, CH), CH, None, False)
    attend_fast(pl.multiple_of(ts, LANES), TAIL, covered, True)
    lmin = jnp.float32(jnp.inf)
    for hp in range(H_A // 2):
        l2 = jnp.sum(ls2_ref[hp], axis=1, keepdims=True)
        lmin = jnp.minimum(lmin, jnp.min(l2))
        acc = acc2_ref[hp]
        o_ref[:, hp * LANES:(hp + 1) * LANES] = jnp.where(left, acc[0:tq] / l2[0:tq], acc[tq:2 * tq] / l2[tq:2 * tq])
    underflow = jnp.logical_not(lmin > 1e-30)

    @pl.when(underflow)
    def _():
        m_ref[...] = jnp.full(m_ref.shape, NEG, F32)
        l_ref[...] = jnp.zeros(l_ref.shape, F32)
        acc2_ref[...] = jnp.zeros(acc2_ref.shape, F32)

        def attend_exact(st, width, extra, with_dz):
            mb2 = mask_bias(st, width, extra)
            for hp in range(H_A // 2):
                s = logits(hp, st, width, mb2, with_dz)
                m_old = m_ref[hp]
                m_new = jnp.maximum(m_old, jnp.max(s, axis=1, keepdims=True))
                alpha = jnp.exp2(m_old - m_new)
                p = jnp.exp2(s - m_new)
                l_ref[hp] = alpha * l_ref[hp] + jnp.sum(p, axis=1, keepdims=True)
                m_ref[hp] = m_new
                vv = v_ref[pl.ds(st, width), hp * LANES:(hp + 1) * LANES]
                acc2_ref[hp] = alpha * acc2_ref[hp] + jnp.dot(p.astype(BF16), vv, preferred_element_type=F32)

        def far_exact(jc, carry):
            attend_exact(pl.multiple_of(jc * CH, CH), CH, None, False)
            return carry

        lax.fori_loop(0, nfar, far_exact, 0)
        attend_exact(pl.multiple_of(ts, LANES), TAIL, covered, True)
        for hp in range(H_A // 2):
            acc = acc2_ref[hp]
            l2 = l_ref[hp]
            o_ref[:, hp * LANES:(hp + 1) * LANES] = jnp.where(left, acc[0:tq] / l2[0:tq],
                                                              acc[tq:2 * tq] / l2[tq:2 * tq])


def _dsa_prompt(pr, rel_bias, nb, seq, tq):
    topk = min(TOPK_MAX, seq // 4)
    assert seq % (2 * KEY_CHUNK) == 0 and seq >= TAIL + KEY_CHUNK and tq == LANES
    wtab = KEY_CHUNK + TAIL
    dmin, dmax = KEY_CHUNK - (wtab - 1), KEY_CHUNK + tq - 1
    g = (_bias_delta(rel_bias, jnp.arange(dmax, dmin - 1, -1, dtype=I32)) * LOG2E).T
    dz = jnp.stack([g[:, tq - 1 - r:tq - 1 - r + wtab] for r in range(tq)], axis=1)
    dzmax = jnp.broadcast_to(jnp.max(g, axis=1).reshape(H_A, 1, 1), (H_A, 1, LANES))
    nq = seq // tq
    rowq = lambda w: pl.BlockSpec((tq, w), lambda b, i: (b * nq + i, 0))
    perb = lambda r, c: pl.BlockSpec((None, r, c), lambda b, i: (b, 0, 0), pipeline_mode=pl.Buffered(1))
    vb = pr["avb"].reshape(nb, seq, W_A)
    return pl.pallas_call(
        functools.partial(_dsa_prompt_kernel, tq=tq, topk=topk),
        grid=(nb, nq),
        in_specs=[rowq(H_I * D_I), rowq(LANES), rowq(LANES), rowq(W_A),
                  perb(D_I, seq), perb(W_A, seq), perb(seq, W_A), _const_spec((H_A, tq, wtab)),
                  _const_spec((H_A, 1, LANES))],
        out_specs=rowq(W_A),
        out_shape=jax.ShapeDtypeStruct((nb * seq, W_A), F32),
        scratch_shapes=[pltpu.VMEM((tq, seq), I32),
                        pltpu.VMEM((H_A // 2, 2 * tq, LANES), BF16),
                        pltpu.VMEM((H_A // 2, 2 * tq, 1), F32),
                        pltpu.VMEM((H_A, 1, LANES), F32),
                        pltpu.VMEM((H_A // 2, 2 * tq, 1), F32), pltpu.VMEM((H_A // 2, 2 * tq, 1), F32),
                        pltpu.VMEM((H_A // 2, 2 * tq, LANES), F32),
                        pltpu.VMEM((H_A // 2, 2 * tq, LANES), F32)],
        compiler_params=_cparams(("arbitrary", "arbitrary")),
        name="dsa_prompt",
    )(pr["qs"], pr["lo"], pr["hi"], pr["aq"], pr["iktb"], pr["ktb"], vb, dz, dzmax)


def _dsa_sample_kernel(pt_ref, qs_ref, lo_ref, hi_ref, aq_ref, ikn_ref, akn_ref, avn_ref, dz_ref,
                       cki_ref, ck_ref, cv_ref, o_ref,
                       kib, kbuf, vbuf, buf_ref, sem_ki, sem_kv, *, steps, topk, n_pages, page, cpp):
    b = pl.program_id(0)
    nb = pl.num_programs(0)
    past = n_pages * page
    nch = n_pages // cpp
    ckeys = cpp * page
    width = past + LANES
    rows_a = H_A * steps

    def ki_copy(seq_i, slot, p):
        return pltpu.make_async_copy(cki_ref.at[pt_ref[seq_i, p]], kib.at[slot, p], sem_ki.at[slot])

    def kv_copies(seq_i, c, slot, p):
        pg = pt_ref[seq_i, c * cpp + p]
        return (pltpu.make_async_copy(ck_ref.at[pg], kbuf.at[slot, p], sem_kv.at[slot]),
                pltpu.make_async_copy(cv_ref.at[pg], vbuf.at[slot, p], sem_kv.at[slot]))

    def start_ki(seq_i, slot):
        def f(p, carry):
            ki_copy(seq_i, slot, p).start()
            return carry
        lax.fori_loop(0, n_pages, f, 0)

    n_slots = kbuf.shape[0]
    ahead = n_slots - 1
    total_chunks = nb * nch

    def start_kv(g):
        @pl.when(g < total_chunks)
        def _():
            def f(p, carry):
                for cp in kv_copies(g // nch, g % nch, g % n_slots, p):
                    cp.start()
                return carry
            lax.fori_loop(0, cpp, f, 0)

    def wait_kv(g):
        def f(p, carry):
            for cp in kv_copies(g // nch, g % nch, g % n_slots, p):
                cp.wait()
            return carry
        lax.fori_loop(0, cpp, f, 0)

    slot_b = b % 2

    @pl.when(b == 0)
    def _():
        start_ki(0, 0)
        for g0 in range(ahead):
            start_kv(jnp.int32(g0))

    def wait_ki(p, carry):
        ki_copy(b, slot_b, p).wait()
        return carry
    lax.fori_loop(0, n_pages, wait_ki, 0)

    @pl.when(b + 1 < nb)
    def _():
        start_ki(b + 1, 1 - slot_b)

    qs = qs_ref[...]
    lo = lo_ref[...]
    hi = hi_ref[...]
    qs_hm = jnp.concatenate([qs[:, h * D_I:(h + 1) * D_I] for h in range(H_I)], axis=0)
    lo_hm = jnp.concatenate([lo[:, SMALL_IW + h:SMALL_IW + h + 1] for h in range(H_I)], axis=0)
    hi_hm = jnp.concatenate([hi[:, SMALL_IW + h:SMALL_IW + h + 1] for h in range(H_I)], axis=0)

    def head_sum(a):
        t = jnp.maximum(jnp.minimum(a, hi_hm), lo_hm)
        return sum(t[h * steps:(h + 1) * steps] for h in range(H_I))

    def score_page(p, carry):
        a = jnp.dot(qs_hm, kib[slot_b, p].astype(BF16), preferred_element_type=F32)
        buf_ref[:, pl.ds(pl.multiple_of(p * page, page), page)] = _sortable(head_sum(a))
        return carry
    lax.fori_loop(0, n_pages, score_page, 0, unroll=4)

    ikn = jnp.concatenate([ikn_ref[...], jnp.zeros((LANES - steps, D_I), F32)], axis=0).astype(BF16)
    a_new = lax.dot_general(qs_hm, ikn, _NT, preferred_element_type=F32)
    t_i = lax.broadcasted_iota(I32, (steps, LANES), 0)
    j_i = lax.broadcasted_iota(I32, (steps, LANES), 1)
    buf_ref[:, past:width] = jnp.where(j_i <= t_i, _sortable(head_sum(a_new)), INT_MIN)

    def count_ge(cand):
        return jnp.sum(jnp.where(buf_ref[...] >= cand, 1.0, 0.0), axis=1, keepdims=True)

    nvalid = past + 1 + lax.broadcasted_iota(I32, (steps, 1), 0)
    short = nvalid <= topk
    keys = buf_ref[...]
    kmax = _sortable(jnp.max(_key_value(jnp.maximum(keys, KEY_NEG_INF)), axis=1, keepdims=True))
    kmin = _sortable(jnp.min(_key_value(jnp.where(keys == INT_MIN, KEY_POS_INF, keys)), axis=1, keepdims=True))
    thr, tie = _kth_largest_search(count_ge, jnp.float32(width), short, topk, steps, kmin, kmax)
    any_tie = jnp.max(jnp.where(tie, 1.0, 0.0)) > 0.5

    @pl.when(jnp.logical_not(any_tie))
    def _():
        buf_ref[...] = jnp.where(buf_ref[...] >= thr, 0, NEG_BITS)

    @pl.when(any_tie)
    def _():
        need = topk - count_ge(thr + 1)
        TW = LANES
        upper = (lax.broadcasted_iota(I32, (TW, TW), 0) < lax.broadcasted_iota(I32, (TW, TW), 1))
        upper = jnp.where(upper, 1.0, 0.0).astype(BF16)

        def fin(c, seen):
            st = pl.multiple_of(c * TW, TW)
            blk = buf_ref[:, pl.ds(st, TW)]
            eq = jnp.where(blk == thr, 1.0, 0.0)
            before = jnp.dot(eq.astype(BF16), upper, preferred_element_type=F32) + seen
            sel = jnp.logical_or(blk > thr, jnp.logical_and(blk == thr, before < need))
            buf_ref[:, pl.ds(st, TW)] = jnp.where(sel, 0, NEG_BITS)
            return seen + jnp.sum(eq, axis=1, keepdims=True)
        lax.fori_loop(0, width // TW, fin, jnp.zeros((steps, 1), F32))

    r_head = lax.broadcasted_iota(I32, (rows_a, W_A), 0) // steps
    c_head = lax.broadcasted_iota(I32, (rows_a, W_A), 1) // D_HA
    own = r_head == c_head
    aq_t = jnp.concatenate([aq_ref[...]] * H_A, axis=0)
    q_bd = jnp.where(own, aq_t, jnp.zeros_like(aq_t))
    tile_rows = lambda a: jnp.concatenate([a] * H_A, axis=0)

    def softmax_step(state, s, pv_fn):
        m_old, l_old, acc = state
        m_new = jnp.maximum(m_old, jnp.max(s, axis=1, keepdims=True))
        alpha = jnp.exp(m_old - m_new)
        p = jnp.exp(s - m_new)
        return m_new, alpha * l_old + jnp.sum(p, axis=1, keepdims=True), alpha * acc + pv_fn(p.astype(BF16))

    def chunk_step(c, state, last):
        g = b * nch + c
        slot = g % n_slots
        wait_kv(g)
        start_kv(g + ahead)

        s = jnp.concatenate([jnp.dot(q_bd, kbuf[slot, p].astype(BF16), preferred_element_type=F32)
                             for p in range(cpp)], axis=1)
        s = s + tile_rows(pltpu.bitcast(buf_ref[:, pl.ds(pl.multiple_of(c * ckeys, ckeys), ckeys)], F32))
        if last:
            s = s + jnp.concatenate([jnp.zeros((rows_a, ckeys - page), F32), dz_ref[:, 0:page]], axis=1)

        def pv(pb):
            return sum(lax.dot_general(pb[:, p * page:(p + 1) * page], vbuf[slot, p].astype(BF16), _NT,
                                       preferred_element_type=F32) for p in range(cpp))
        return softmax_step(state, s, pv)

    state = (jnp.full((rows_a, 1), NEG, F32), jnp.zeros((rows_a, 1), F32), jnp.zeros((rows_a, W_A), F32))
    state = lax.fori_loop(0, nch - 1, lambda c, st: chunk_step(c, st, False), state)
    state = chunk_step(nch - 1, state, True)
    pad = jnp.zeros((LANES - steps, W_A), F32)
    kn = jnp.concatenate([akn_ref[...], pad], axis=0).astype(BF16)
    vn = jnp.concatenate([avn_ref[...], pad], axis=0).astype(BF16)
    s_new = (lax.dot_general(q_bd, kn, _NT, preferred_element_type=F32)
             + tile_rows(pltpu.bitcast(buf_ref[:, past:width], F32)) + dz_ref[:, page:2 * page])
    _, l_fin, acc = softmax_step(state, s_new, lambda pb: jnp.dot(pb, vn, preferred_element_type=F32))
    accn = jnp.where(own, acc / l_fin, 0.0)
    o_ref[...] = sum(accn[h * steps:(h + 1) * steps] for h in range(H_A))


def _dsa_sample(pr, cache_k, cache_v, cache_k_idx, page_table, rel_bias, nseq, steps):
    n_pool, page = cache_k.shape[0], cache_k.shape[1]
    n_pages = page_table.shape[1]
    past = n_pages * page
    topk = min(TOPK_MAX, (past + steps) // 4)
    cpp = 8 if n_pages % 8 == 0 else n_pages
    assert page == LANES and steps <= LANES
    t = jnp.arange(steps, dtype=I32)[:, None]
    c = jnp.arange(LANES, dtype=I32)[None, :]
    dz = jnp.transpose(_bias_delta(rel_bias, jnp.concatenate([t + LANES - c, t - c], axis=1)), (2, 0, 1))
    dz = dz.reshape(H_A * steps, 2 * LANES)
    ck_t = jnp.transpose(cache_k, (0, 2, 3, 1)).reshape(n_pool, W_A, page)
    cv_t = jnp.transpose(cache_v, (0, 2, 3, 1)).reshape(n_pool, W_A, page)
    cki_t = jnp.transpose(cache_k_idx, (0, 2, 1))
    rows = lambda w: pl.BlockSpec((steps, w), lambda b, pt: (b, 0))
    anyspec = pl.BlockSpec(memory_space=pl.ANY)
    return pl.pallas_call(
        functools.partial(_dsa_sample_kernel, steps=steps, topk=topk, n_pages=n_pages, page=page, cpp=cpp),
        grid_spec=pltpu.PrefetchScalarGridSpec(
            num_scalar_prefetch=1,
            grid=(nseq,),
            in_specs=[rows(H_I * D_I), rows(LANES), rows(LANES), rows(W_A), rows(D_I), rows(W_A), rows(W_A),
                      pl.BlockSpec((H_A * steps, 2 * LANES), lambda b, pt: (0, 0)),
                      anyspec, anyspec, anyspec],
            out_specs=rows(W_A),
            scratch_shapes=[pltpu.VMEM((2, n_pages, D_I, page), F32),
                            pltpu.VMEM((KV_SLOTS, cpp, W_A, page), F32), pltpu.VMEM((KV_SLOTS, cpp, W_A, page), F32),
                            pltpu.VMEM((steps, past + LANES), I32),
                            pltpu.SemaphoreType.DMA((2,)), pltpu.SemaphoreType.DMA((KV_SLOTS,))],
        ),
        out_shape=jax.ShapeDtypeStruct((nseq * steps, W_A), F32),
        compiler_params=_cparams(("arbitrary",)),
        name="dsa_sample",
    )(page_table.astype(I32), pr["qs"], pr["lo"], pr["hi"], pr["aq"], pr["ik"], pr["ak"], pr["av"], dz,
      cki_t, ck_t, cv_t)


def _layer_norm(y, g, b):
    mu = jnp.mean(y, axis=1, keepdims=True)
    var = jnp.mean(jnp.square(y - mu), axis=1, keepdims=True)
    return (y - mu) * lax.rsqrt(var + LN_EPS) * g + b


def _mix_router_kernel(x_ref, mo_ref, ao_ref, wo1_ref, wo2_ref, g_ref, b_ref, rw_ref, rb_ref,
                       h_ref, route_ref, gate_ref, cnt_ref, carry_s):
    @pl.when(pl.program_id(0) == 0)
    def _():
        carry_s[...] = jnp.zeros_like(carry_s)

    tm = x_ref.shape[0]
    mix = (jnp.dot(mo_ref[...], wo1_ref[...], preferred_element_type=F32)
           + jnp.dot(ao_ref[...].astype(BF16), wo2_ref[...], preferred_element_type=F32))
    h = _layer_norm(DN_ALPHA * x_ref[...] + mix, g_ref[...], b_ref[...])
    h_ref[...] = h
    logits = jnp.dot(h.astype(BF16), rw_ref[...], preferred_element_type=F32) + rb_ref[...]
    lane = lax.broadcasted_iota(I32, (tm, LANES), 1)
    lane_f = lane.astype(F32)
    l = jnp.where(lane < N_EXPERTS, logits, -jnp.inf)
    vals, hots, idxs = [], [], []
    member = jnp.zeros((tm, LANES), F32)
    for _ in range(TOP_K_EXP):
        mx = jnp.max(l, axis=1, keepdims=True)
        idx = jnp.min(jnp.where(l == mx, lane_f, float(LANES)), axis=1, keepdims=True)
        hot = lane_f == idx
        vals.append(mx)
        idxs.append(idx)
        hots.append(hot)
        member = member + jnp.where(hot, 1.0, 0.0)
        l = jnp.where(hot, -jnp.inf, l)
    ex = [jnp.exp(v - vals[0]) for v in vals]
    tot = sum(ex)
    stril = (lax.broadcasted_iota(I32, (tm, tm), 1) < lax.broadcasted_iota(I32, (tm, tm), 0))
    before = jnp.dot(jnp.where(stril, 1.0, 0.0).astype(BF16), member.astype(BF16),
                     preferred_element_type=F32) + carry_s[...]
    route = jnp.zeros((tm, LANES), F32)
    gates = jnp.zeros((tm, LANES), F32)
    for k in range(TOP_K_EXP):
        rank = jnp.sum(jnp.where(hots[k], before, 0.0), axis=1, keepdims=True)
        route = route + jnp.where(lane == k, idxs[k], 0.0) + jnp.where(lane == TOP_K_EXP + k, rank, 0.0)
        gates = gates + jnp.where(lane == k, ex[k] / tot, 0.0)
    route_ref[...] = route.astype(I32)
    gate_ref[...] = gates
    carry_s[...] = carry_s[...] + jnp.sum(member, axis=0, keepdims=True)
    cnt_ref[...] = carry_s[...]


def _scatter_rows_kernel(zstart_ref, dest_ref, h_ref, xs_out, zbuf, sem, zsem):
    ts = h_ref.shape[0]

    @pl.when(pl.program_id(0) == 0)
    def _():
        zbuf[...] = jnp.zeros_like(zbuf)

        def zero_block(j, go):
            @pl.when(zstart_ref[j] >= 0)
            def _():
                rows = pl.ds(pl.multiple_of(zstart_ref[j], ROW_BLOCK), ROW_BLOCK)
                go(pltpu.make_async_copy(zbuf, xs_out.at[rows], zsem))
        for j in range(2 * N_EXPERTS):
            zero_block(j, lambda cp: cp.start())
        for j in range(2 * N_EXPERTS):
            zero_block(j, lambda cp: cp.wait())

    def start(t, carry):
        for k in range(TOP_K_EXP):
            d = dest_ref[t * TOP_K_EXP + k]
            pltpu.make_async_copy(h_ref.at[pl.ds(t, 1)], xs_out.at[pl.ds(d, 1)], sem).start(priority=k % 2)
        return carry
    lax.fori_loop(0, ts, start, 0)

    def wait(t, carry):
        for k in range(TOP_K_EXP):
            pltpu.make_async_copy(h_ref.at[pl.ds(0, 1)], xs_out.at[pl.ds(0, 1)], sem).wait()
        return carry
    lax.fori_loop(0, ts, wait, 0)


def _expert_kernel(be_ref, na_ref, xs_ref, wgu_ref, bgu_ref, wdn_ref, bdn_ref, ys_ref):
    del be_ref
    active = pl.program_id(0) < na_ref[0]

    @pl.when(jnp.logical_not(active))
    def _():
        ys_ref[...] = jnp.zeros_like(ys_ref)

    @pl.when(active)
    def _():
        f = wdn_ref.shape[0]
        gu = jnp.dot(xs_ref[...].astype(BF16), wgu_ref[...], preferred_element_type=F32) + bgu_ref[...]
        gt = jnp.minimum(gu[:, :f], SWIGLU_LIMIT)
        up = jnp.clip(gu[:, f:], -SWIGLU_LIMIT, SWIGLU_LIMIT)
        act = (up + 1.0) * gt / (1.0 + jnp.exp(-SWIGLU_ALPHA * gt))
        ys_ref[...] = jnp.dot(act.astype(BF16), wdn_ref[...], preferred_element_type=F32) + bdn_ref[...]


def _combine_kernel(dest_ref, dest_next_ref, gate_ref, h_ref, g_ref, b_ref, ys_ref, o_ref, ybuf, sem):
    tc = h_ref.shape[0]
    i = pl.program_id(0)
    slot = i % 2

    def copy(s, t, k, d):
        return pltpu.make_async_copy(ys_ref.at[pl.ds(d, 1)], ybuf.at[s, k, pl.ds(t, 1)], sem.at[s])

    def start_tile(idx_ref, s):
        def start(t, carry):
            for k in range(TOP_K_EXP):
                copy(s, t, k, idx_ref[t * TOP_K_EXP + k]).start(priority=k % 2)
            return carry
        lax.fori_loop(0, tc, start, 0)

    @pl.when(i == 0)
    def _():
        start_tile(dest_ref, 0)

    @pl.when(i + 1 < pl.num_programs(0))
    def _():
        start_tile(dest_next_ref, 1 - slot)

    def wait(t, carry):
        for k in range(TOP_K_EXP):
            copy(slot, 0, k, 0).wait()
        return carry
    lax.fori_loop(0, tc, wait, 0)
    gates = gate_ref[...]
    y = sum(gates[:, k:k + 1] * ybuf[slot, k] for k in range(TOP_K_EXP))
    o_ref[...] = _layer_norm(DN_ALPHA * h_ref[...] + y, g_ref[...], b_ref[...])


def _finish(x2, m_out, a_out, ffn, tm, tr):
    n, d = x2.shape
    wo1, wo2, ln1_g, ln1_b, rw, rb, wgu, bgu, wdn, bdn, ln2_g, ln2_b = ffn
    f = wdn.shape[1]
    vec = lambda a: a.reshape(1, -1).astype(F32)
    row = lambda t, w: pl.BlockSpec((t, w), lambda i, *_: (i, 0))
    sds = jax.ShapeDtypeStruct
    h, route, gates, cnt = pl.pallas_call(
        _mix_router_kernel,
        grid=(n // tm,),
        in_specs=[row(tm, d), row(tm, W_M), row(tm, W_A), _const_spec(wo1.shape), _const_spec(wo2.shape),
                  _const_spec((1, d)), _const_spec((1, d)), _const_spec(rw.shape), _const_spec((1, LANES))],
        out_specs=[row(tm, d), row(tm, LANES), row(tm, LANES), pl.BlockSpec((1, LANES), lambda i: (0, 0))],
        out_shape=[sds((n, d), F32), sds((n, LANES), I32), sds((n, LANES), F32), sds((1, LANES), F32)],
        scratch_shapes=[pltpu.VMEM((1, LANES), F32)],
        compiler_params=_cparams(("arbitrary",)),
        name="mix_router",
    )(x2, m_out, a_out, wo1, wo2, vec(ln1_g), vec(ln1_b), rw, rb)

    counts = cnt[0, :N_EXPERTS].astype(I32)
    padded = (counts + ROW_BLOCK - 1) // ROW_BLOCK * ROW_BLOCK
    pad_end = jnp.cumsum(padded)
    pad_start = pad_end - padded
    dest = (pad_start[route[:, :TOP_K_EXP]] + route[:, TOP_K_EXP:2 * TOP_K_EXP]).reshape(-1)
    n_blk = -(-n * TOP_K_EXP // ROW_BLOCK) + N_EXPERTS
    blk_start = jnp.arange(n_blk, dtype=I32) * ROW_BLOCK
    blk_e = jnp.minimum(jnp.sum((pad_end[None, :] <= blk_start[:, None]).astype(I32), axis=1), N_EXPERTS - 1)
    n_act = (pad_end[-1:] // ROW_BLOCK).astype(I32)
    tail_rows = pad_end[-1] + jnp.arange(N_EXPERTS, dtype=I32) * ROW_BLOCK
    zero_blocks = jnp.concatenate([jnp.where(counts > 0, pad_end - ROW_BLOCK, -1),
                                   jnp.where(tail_rows < n_blk * ROW_BLOCK, tail_rows, -1)]).astype(I32)

    n_steps = n // tr
    smem_dest = pl.BlockSpec((tr * TOP_K_EXP,), lambda i, *_: (i,), memory_space=pltpu.SMEM)
    smem_next = pl.BlockSpec((tr * TOP_K_EXP,), lambda i, *_: (jnp.minimum(i + 1, n_steps - 1),),
                             memory_space=pltpu.SMEM)
    anyspec = pl.BlockSpec(memory_space=pl.ANY)
    xs = pl.pallas_call(
        _scatter_rows_kernel,
        grid_spec=pltpu.PrefetchScalarGridSpec(
            num_scalar_prefetch=1,
            grid=(n_steps,),
            in_specs=[smem_dest, row(tr, d)],
            out_specs=anyspec,
            scratch_shapes=[pltpu.VMEM((ROW_BLOCK, d), F32), pltpu.SemaphoreType.DMA(()),
                            pltpu.SemaphoreType.DMA(())],
        ),
        out_shape=sds((n_blk * ROW_BLOCK, d), F32),
        compiler_params=_cparams(("arbitrary",)),
        name="moe_scatter",
    )(zero_blocks, dest, h)

    blk = lambda i, be, na: jnp.minimum(i, na[0] - 1)
    ys = pl.pallas_call(
        _expert_kernel,
        grid_spec=pltpu.PrefetchScalarGridSpec(
            num_scalar_prefetch=2,
            grid=(n_blk,),
            in_specs=[pl.BlockSpec((ROW_BLOCK, d), lambda i, be, na: (blk(i, be, na), 0)),
                      pl.BlockSpec((None, d, 2 * f), lambda i, be, na: (be[i], 0, 0)),
                      pl.BlockSpec((None, 1, 2 * f), lambda i, be, na: (be[i], 0, 0)),
                      pl.BlockSpec((None, f, d), lambda i, be, na: (be[i], 0, 0)),
                      pl.BlockSpec((None, 1, d), lambda i, be, na: (be[i], 0, 0))],
            out_specs=pl.BlockSpec((ROW_BLOCK, d), lambda i, be, na: (i, 0)),
        ),
        out_shape=sds((n_blk * ROW_BLOCK, d), F32),
        compiler_params=_cparams(("arbitrary",)),
        name="moe_experts",
    )(blk_e, n_act, xs, wgu, bgu, wdn, bdn)

    return pl.pallas_call(
        _combine_kernel,
        grid=(n_steps,),
        in_specs=[smem_dest, smem_next, row(tr, LANES), row(tr, d), _const_spec((1, d)), _const_spec((1, d)),
                  anyspec],
        out_specs=row(tr, d),
        out_shape=sds((n, d), F32),
        scratch_shapes=[pltpu.VMEM((2, TOP_K_EXP, tr, d), F32), pltpu.SemaphoreType.DMA((2,))],
        compiler_params=_cparams(("arbitrary",)),
        name="moe_combine",
    )(dest, dest, gates, h, vec(ln2_g), vec(ln2_b), ys)


def _pack_ffn(w_out, ln1_g, ln1_b, router_w, router_b, w_gu, b_gu, w_dn, b_dn, ln2_g, ln2_b):
    d = w_out.shape[1]
    rw = jnp.pad(router_w, ((0, 0), (0, LANES - N_EXPERTS))).astype(BF16)
    rb = jnp.pad(router_b.astype(F32), (0, LANES - N_EXPERTS)).reshape(1, LANES)
    return (w_out[:W_M].astype(BF16), w_out[W_M:].astype(BF16), ln1_g, ln1_b, rw, rb,
            w_gu.astype(BF16), b_gu.astype(F32).reshape(N_EXPERTS, 1, -1),
            w_dn.astype(BF16), b_dn.astype(F32).reshape(N_EXPERTS, 1, d), ln2_g, ln2_b)


def _tile(n, pref):
    t = pref
    while n % t:
        t //= 2
    return t


def kernel(x_prompt, x_sample, cache_k, cache_v, cache_k_idx, state_C, state_n, state_m, page_table,
           w_in, b_ig, b_fg, mlstm_norm_g, w_out, ln1_g, ln1_b, router_w, router_b,
           w_gate_up, b_gate_up, w_down, b_down, ln2_g, ln2_b, rel_bias):
    assert w_in.shape[0] == DEPTH
    nb, seq, d = x_prompt.shape
    ns, steps, _ = x_sample.shape
    bias = _gate_bias(b_ig[0], b_fg[0])
    ffn = _pack_ffn(w_out[0], ln1_g[0], ln1_b[0], router_w[0], router_b[0], w_gate_up[0], b_gate_up[0],
                    w_down[0], b_down[0], ln2_g[0], ln2_b[0])

    xp = x_prompt.reshape(nb * seq, d)
    pp = _project(xp, w_in[0], nb, seq, _tile(seq, 256), True)
    mo_p, c_p, n_p, m_p = _mlstm_prompt(pp, bias, mlstm_norm_g[0], nb, seq, LANES)
    ao_p = _dsa_prompt(pp, rel_bias, nb, seq, LANES)
    y_p = _finish(xp, mo_p, ao_p, ffn, _tile(nb * seq, 512), _tile(nb * seq, 256))

    xs = x_sample.reshape(ns * steps, d)
    ps = _project(xs, w_in[0], 1, ns * steps, _tile(ns * steps, 256), False)
    mo_s, c_s, n_s, m_s = _mlstm_sample(ps, bias, mlstm_norm_g[0], state_C[0], state_n[0], state_m[0], ns, steps)
    ao_s = _dsa_sample(ps, cache_k[0], cache_v[0], cache_k_idx[0], page_table, rel_bias, ns, steps)
    y_s = _finish(xs, mo_s, ao_s, ffn, _tile(ns * steps, 512), _tile(ns * steps, 256))

    dt = x_prompt.dtype
    L = lambda a: a[None]
    heads = lambda a: jnp.transpose(a.reshape(nb, H_A, D_HA, seq), (0, 3, 1, 2))
    return (y_p.reshape(nb, seq, d), y_s.reshape(ns, steps, d),
            L(heads(pp["ktf"])), L(heads(pp["vtf"])), L(jnp.transpose(pp["iktf"], (0, 2, 1))),
            L(c_p.astype(dt)), L(n_p.astype(dt)), L(m_p.astype(dt)),
            L(ps["ak"].reshape(ns, steps, H_A, D_HA)), L(ps["av"].reshape(ns, steps, H_A, D_HA)),
            L(ps["ik"].reshape(ns, steps, D_I)),
            L(c_s.astype(state_C.dtype)), L(n_s.astype(state_n.dtype)), L(m_s.astype(state_m.dtype)))
```

```python
import functools
import math

import numpy as np
import jax
import jax.numpy as jnp
from jax import lax
from jax.experimental import pallas as pl
from jax.experimental.pallas import tpu as pltpu

F32, BF16, I32 = jnp.float32, jnp.bfloat16, jnp.int32

H_M, D_HM = 4, 128
H_A, D_HA = 8, 64
H_I, D_I = 4, 64
W_M, W_A = H_M * D_HM, H_A * D_HA
TOPK_MAX = 256
N_BUCKETS, MAX_DISTANCE = 32, 128
N_EXPERTS, TOP_K_EXP = 32, 4
SWIGLU_LIMIT, SWIGLU_ALPHA = 7.0, 1.702
GATE_CAP = 15.0
LN_EPS = 1e-5
DEPTH = 1
DN_ALPHA = (2.0 * DEPTH) ** 0.25

LANES = 128
VMEM_LIMIT_BYTES = 56 * 1024 * 1024
LOG2E = 1.4426950408889634
NEG = -1e30
NEG_BITS = int(np.float32(NEG).view(np.int32))
INT_MIN = -2 ** 31
KEY_POS_INF = 0x7F800000
KEY_NEG_INF = (0xFF800000 ^ 0x7FFFFFFF) - 2 ** 32
KEY_CHUNK = 512
TAIL = KEY_CHUNK + LANES
ROW_BLOCK = 512
BRACKET_ITERS = 24
KV_SLOTS = 4
SMALL_IK, SMALL_MI, SMALL_MF, SMALL_IW = 0, 64, 68, 72

_NT = (((1,), (1,)), ((), ()))


def _cparams(sem):
    return pltpu.CompilerParams(dimension_semantics=sem, vmem_limit_bytes=VMEM_LIMIT_BYTES)


def _const_spec(shape):
    zeros = (0,) * len(shape)
    return pl.BlockSpec(shape, lambda *_: zeros, pipeline_mode=pl.Buffered(1))


_PROJ_COMMON = ("mq", "mk", "mv", "mo", "sm", "aq", "qs", "lo", "hi")
_PROJ_EXTRA = {
    True: ("avb", "ktb", "iktb", "ktf", "vtf", "iktf"),
    False: ("ak", "av", "ik"),
}


def _proj_kernel(x_ref, wm_ref, wa_ref, wi_ref, ws_ref, wt_ref, *out_refs, transposed):
    o = dict(zip(_PROJ_COMMON + _PROJ_EXTRA[transposed], out_refs))
    xb = x_ref[...].astype(BF16)
    ym = jnp.dot(xb, wm_ref[...], preferred_element_type=F32)
    o["mq"][...] = ym[:, 0:W_M].astype(BF16)
    o["mk"][...] = (ym[:, W_M:2 * W_M] * (D_HM ** -0.5)).astype(BF16)
    o["mv"][...] = ym[:, 2 * W_M:3 * W_M].astype(BF16)
    o["mo"][...] = ym[:, 3 * W_M:4 * W_M]
    ya = jnp.dot(xb, wa_ref[...], preferred_element_type=F32)
    o["aq"][...] = (ya[:, 0:W_A] * (D_HA ** -0.5 * (LOG2E if transposed else 1.0))).astype(BF16)
    xl = (x_ref[...] - xb.astype(F32)).astype(BF16)
    ys = (jnp.dot(xb, ws_ref[0], preferred_element_type=F32) + jnp.dot(xl, ws_ref[0], preferred_element_type=F32)
          + jnp.dot(xb, ws_ref[1], preferred_element_type=F32))
    o["sm"][...] = ys
    yi = jnp.dot(xb, wi_ref[...], preferred_element_type=F32)
    wsc = ys * (H_I ** -0.5 * D_I ** -0.5)
    lane = lax.broadcasted_iota(I32, yi.shape, 1)
    scale = wsc[:, SMALL_IW + 3:SMALL_IW + 4]
    for h in (2, 1, 0):
        scale = jnp.where(lane < (h + 1) * D_I, wsc[:, SMALL_IW + h:SMALL_IW + h + 1], scale)
    o["qs"][...] = (yi * scale).astype(BF16)
    o["lo"][...] = jnp.where(ys >= 0.0, 0.0, -jnp.inf)
    o["hi"][...] = jnp.where(ys <= 0.0, 0.0, jnp.inf)
    if transposed:
        o["avb"][...] = ya[:, W_A:2 * W_A].astype(BF16)
        yt = lax.dot_general(wt_ref[...], xb, _NT, preferred_element_type=F32)
        o["ktf"][...] = yt[0:W_A]
        o["ktb"][...] = yt[0:W_A].astype(BF16)
        o["vtf"][...] = yt[W_A:2 * W_A]
        o["iktf"][...] = yt[2 * W_A:2 * W_A + D_I]
        o["iktb"][...] = yt[2 * W_A:2 * W_A + D_I].astype(BF16)
    else:
        o["ak"][...] = ya[:, W_A:2 * W_A]
        o["av"][...] = ya[:, 2 * W_A:3 * W_A]
        o["ik"][...] = ys[:, SMALL_IK:SMALL_IK + D_I]


def _pack_w_in(w_in, transposed):
    d = w_in.shape[0]
    c = np.cumsum([0, W_M, W_M, W_M, W_M, H_M, H_M, W_A, W_A, W_A, H_I * D_I, D_I, H_I])
    seg = lambda i: w_in[:, c[i]:c[i + 1]]
    wm = w_in[:, c[0]:c[4]]
    wa = jnp.concatenate([seg(6), seg(8)], axis=1) if transposed else w_in[:, c[6]:c[9]]
    wi = seg(9)
    ws = jnp.concatenate([seg(10), seg(4), seg(5), seg(11),
                          jnp.zeros((d, LANES - D_I - 2 * H_M - H_I), w_in.dtype)], axis=1)
    wt = jnp.concatenate([seg(7), seg(8), seg(10)], axis=1).T
    ws_hi = ws.astype(BF16)
    ws2 = jnp.stack([ws_hi, (ws - ws_hi.astype(F32)).astype(BF16)])
    return (wm.astype(BF16), wa.astype(BF16), wi.astype(BF16), ws2, wt.astype(BF16))


def _project(x2, w_in, nb, seq, tm, transposed):
    n, d = x2.shape
    packs = _pack_w_in(w_in, transposed)
    nps = seq // tm
    sds = jax.ShapeDtypeStruct
    row = lambda w, dt: (sds((n, w), dt), pl.BlockSpec((tm, w), lambda i: (i, 0)))
    tr = lambda w, dt: (sds((nb, w, seq), dt), pl.BlockSpec((None, w, tm), lambda i: (i // nps, 0, i % nps)))
    outs = dict(
        mq=row(W_M, BF16), mk=row(W_M, BF16), mv=row(W_M, BF16), mo=row(W_M, F32), sm=row(LANES, F32),
        aq=row(W_A, BF16), qs=row(H_I * D_I, BF16), lo=row(LANES, F32), hi=row(LANES, F32),
        avb=row(W_A, BF16), ktb=tr(W_A, BF16), iktb=tr(D_I, BF16), ktf=tr(W_A, F32), vtf=tr(W_A, F32),
        iktf=tr(D_I, F32), ak=row(W_A, F32), av=row(W_A, F32), ik=row(D_I, F32),
    )
    names = _PROJ_COMMON + _PROJ_EXTRA[transposed]
    res = pl.pallas_call(
        functools.partial(_proj_kernel, transposed=transposed),
        grid=(n // tm,),
        in_specs=[pl.BlockSpec((tm, d), lambda i: (i, 0))] + [_const_spec(w.shape) for w in packs],
        out_specs=[outs[k][1] for k in names],
        out_shape=[outs[k][0] for k in names],
        compiler_params=_cparams(("arbitrary",)),
        name="proj_t" if transposed else "proj",
    )(x2, *packs)
    return dict(zip(names, res))


def _gates(sm, bias):
    cap = GATE_CAP * jnp.tanh((sm + bias) / GATE_CAP)
    e = jnp.exp(-jnp.abs(cap))
    u = 1.0 + e
    log1p_e = jnp.where(u == 1.0, e, jnp.log(u) * (e / (u - 1.0)))
    logsig = jnp.minimum(cap, 0.0) - log1p_e
    lane = lax.broadcasted_iota(I32, sm.shape, 1)
    return jnp.where(lane >= SMALL_MF, logsig, cap)


def _split3(x):
    a = x.astype(BF16)
    r = x - a.astype(F32)
    b = r.astype(BF16)
    c = (r - b.astype(F32)).astype(BF16)
    return a, b, c


def _dot3(mat, parts):
    return sum(jnp.dot(mat, p, preferred_element_type=F32) for p in parts)


def _headnorm_out(hh, mo, g):
    mu = jnp.mean(hh, axis=1, keepdims=True)
    var = jnp.mean(jnp.square(hh - mu), axis=1, keepdims=True)
    hn = (hh - mu) * lax.rsqrt(var + LN_EPS) * g
    return (hn / (1.0 + jnp.exp(-mo))).astype(BF16)


def _mlstm_prompt_kernel(q_ref, k_ref, v_ref, mo_ref, sm_ref, bias_ref, g_ref,
                         out_ref, c_out, n_out, m_out, c_s, n_s, m_s, *, chunk):
    j = pl.program_id(1)

    @pl.when(j == 0)
    def _():
        c_s[...] = jnp.zeros_like(c_s)
        n_s[...] = jnp.zeros_like(n_s)
        m_s[...] = jnp.zeros_like(m_s)

    L = chunk
    gate = _gates(sm_ref[...], bias_ref[...])
    r_i = lax.broadcasted_iota(I32, (L, L), 0)
    c_i = lax.broadcasted_iota(I32, (L, L), 1)
    tril = c_i <= r_i
    cum = _dot3(jnp.where(tril, 1.0, 0.0).astype(BF16), _split3(gate))
    gate_t = gate.T
    cum_t = cum.T
    for h in range(H_M):
        sl = slice(h * D_HM, (h + 1) * D_HM)
        b_col = cum[:, SMALL_MF + h:SMALL_MF + h + 1]
        b_row = cum_t[SMALL_MF + h:SMALL_MF + h + 1, :]
        i_col = gate[:, SMALL_MI + h:SMALL_MI + h + 1]
        i_row = gate_t[SMALL_MI + h:SMALL_MI + h + 1, :]
        m_prev = m_s[h][:, 0:1]
        dmat = jnp.where(tril, b_col - b_row + i_row, -jnp.inf)
        inter = b_col + m_prev
        mt = jnp.maximum(inter, jnp.max(dmat, axis=1, keepdims=True))
        w_intra = jnp.exp(dmat - mt)
        w_inter = jnp.exp(inter - mt)
        q, k, v = q_ref[:, sl], k_ref[:, sl], v_ref[:, sl]
        s = lax.dot_general(q, k, _NT, preferred_element_type=F32) * w_intra
        c_old = c_s[h]
        n_old = n_s[h]
        num = (jnp.dot(s.astype(BF16), v, preferred_element_type=F32)
               + w_inter * lax.dot_general(q, c_old.astype(BF16), _NT, preferred_element_type=F32))
        den = (jnp.sum(s, axis=1, keepdims=True)
               + w_inter * jnp.sum(q.astype(F32) * n_old, axis=1, keepdims=True))
        hh = num / jnp.maximum(jnp.abs(den), jnp.exp(-mt))
        out_ref[:, sl] = _headnorm_out(hh, mo_ref[:, sl], g_ref[:, sl])
        bl = b_row[:, L - 1:L]
        decay = bl - b_col + i_col
        m_new = jnp.maximum(bl + m_prev, jnp.max(decay, axis=0, keepdims=True))
        ws = jnp.exp(decay - m_new)
        cs = jnp.exp(bl + m_prev - m_new)
        vw_t = (v.astype(F32) * ws).T.astype(BF16)
        c_s[h] = cs * c_old + jnp.dot(vw_t, k, preferred_element_type=F32)
        n_s[h] = cs * n_old + jnp.sum(k.astype(F32) * ws, axis=0, keepdims=True)
        m_s[h] = jnp.broadcast_to(m_new, (1, LANES))

    @pl.when(j == pl.num_programs(1) - 1)
    def _():
        c_out[...] = c_s[...]
        n_out[...] = n_s[...]
        m_out[...] = m_s[...]


def _gate_bias(b_ig, b_fg):
    z = jnp.zeros((LANES,), F32)
    z = z.at[SMALL_MI:SMALL_MI + H_M].set(b_ig.astype(F32)).at[SMALL_MF:SMALL_MF + H_M].set(b_fg.astype(F32))
    return z.reshape(1, LANES)


def _mlstm_prompt(pr, bias, norm_g, nb, seq, chunk):
    n = nb * seq
    nc = seq // chunk
    row = lambda w: pl.BlockSpec((chunk, w), lambda b, j: (b * nc + j, 0))
    st = lambda *s: pl.BlockSpec((None,) + s, lambda b, j: (b,) + (0,) * len(s))
    sds = jax.ShapeDtypeStruct
    out, c, nn, m = pl.pallas_call(
        functools.partial(_mlstm_prompt_kernel, chunk=chunk),
        grid=(nb, nc),
        in_specs=[row(W_M), row(W_M), row(W_M), row(W_M), row(LANES),
                  pl.BlockSpec((1, LANES), lambda b, j: (0, 0)), pl.BlockSpec((1, W_M), lambda b, j: (0, 0))],
        out_specs=[row(W_M), st(H_M, D_HM, D_HM), st(H_M, 1, D_HM), st(H_M, 1, LANES)],
        out_shape=[sds((n, W_M), BF16), sds((nb, H_M, D_HM, D_HM), F32),
                   sds((nb, H_M, 1, D_HM), F32), sds((nb, H_M, 1, LANES), F32)],
        scratch_shapes=[pltpu.VMEM((H_M, D_HM, D_HM), F32), pltpu.VMEM((H_M, 1, D_HM), F32),
                        pltpu.VMEM((H_M, 1, LANES), F32)],
        compiler_params=_cparams(("arbitrary", "arbitrary")),
        name="mlstm_prompt",
    )(pr["mq"], pr["mk"], pr["mv"], pr["mo"], pr["sm"], bias, norm_g.reshape(1, W_M).astype(F32))
    return out, c, nn.reshape(nb, H_M, D_HM), m[:, :, 0, 0]


def _mlstm_sample_kernel(q_ref, k_ref, v_ref, mo_ref, sm_ref, bias_ref, g_ref, mrow_ref, nrow_ref,
                         c_in, n_in, out_ref, c_out, n_out, m_out, numi_s, *, steps):
    h = pl.program_id(1)
    R = q_ref.shape[0]
    nseq = R // steps
    gate = _gates(sm_ref[...], bias_ref[...])
    r_i = lax.broadcasted_iota(I32, (R, R), 0)
    c_i = lax.broadcasted_iota(I32, (R, R), 1)
    same = (r_i // steps) == (c_i // steps)
    mask = jnp.logical_and(same, c_i <= r_i)
    parts = _split3(gate)
    cum = _dot3(jnp.where(mask, 1.0, 0.0).astype(BF16), parts)
    tot = _dot3(jnp.where(same, 1.0, 0.0).astype(BF16), parts)
    lane = lax.broadcasted_iota(I32, (R, LANES), 1)
    sub = lax.broadcasted_iota(I32, (LANES, R), 0)
    col = lambda a, off: jnp.sum(jnp.where(lane == off + h, a, 0.0), axis=1, keepdims=True)
    rowv = lambda a, off: jnp.sum(jnp.where(sub == off + h, a.T, 0.0), axis=0, keepdims=True)
    b_col, b_row = col(cum, SMALL_MF), rowv(cum, SMALL_MF)
    i_col, i_row = col(gate, SMALL_MI), rowv(gate, SMALL_MI)
    bl_col, bl_row = col(tot, SMALL_MF), rowv(tot, SMALL_MF)
    m_col = jnp.sum(jnp.where(lane == h, mrow_ref[...], 0.0), axis=1, keepdims=True)
    dmat = jnp.where(mask, b_col - b_row + i_row, -jnp.inf)
    inter = b_col + m_col
    mt = jnp.maximum(inter, jnp.max(dmat, axis=1, keepdims=True))
    w_intra = jnp.exp(dmat - mt)
    w_inter = jnp.exp(inter - mt)
    q, k, v = q_ref[...], k_ref[...], v_ref[...]
    s = lax.dot_general(q, k, _NT, preferred_element_type=F32) * w_intra
    for b in range(nseq):
        rows = slice(b * steps, (b + 1) * steps)
        numi_s[rows, :] = lax.dot_general(q[rows, :], c_in[b].astype(BF16), _NT, preferred_element_type=F32)
    num = jnp.dot(s.astype(BF16), v, preferred_element_type=F32) + w_inter * numi_s[...]
    den = (jnp.sum(s, axis=1, keepdims=True)
           + w_inter * jnp.sum(q.astype(F32) * nrow_ref[...], axis=1, keepdims=True))
    hh = num / jnp.maximum(jnp.abs(den), jnp.exp(-mt))
    out_ref[...] = _headnorm_out(hh, mo_ref[...], g_ref[...])
    decay_col = bl_col - b_col + i_col
    decay_row = bl_row - b_row + i_row
    seg_max = jnp.max(jnp.where(same, decay_row, -jnp.inf), axis=1, keepdims=True)
    m_new = jnp.maximum(bl_col + m_col, seg_max)
    ws = jnp.exp(decay_col - m_new)
    cs = jnp.exp(bl_col + m_col - m_new)
    kf = k.astype(F32)
    vw_t = (v.astype(F32) * ws).T.astype(BF16)
    kw = kf * ws
    rid = lax.broadcasted_iota(I32, (R, 1), 0) // steps
    for b in range(nseq):
        r0 = b * steps
        kb = jnp.where(rid == b, kf, 0.0).astype(BF16)
        cs_b = cs[r0:r0 + 1, :]
        c_out[b] = cs_b * c_in[b] + jnp.dot(vw_t, kb, preferred_element_type=F32)
        n_out[b] = cs_b * n_in[b] + jnp.sum(kw[r0:r0 + steps, :], axis=0, keepdims=True)
        m_out[b] = jnp.broadcast_to(m_new[r0:r0 + 1, :], (1, LANES))


def _mlstm_sample(pr, bias, norm_g, state_c, state_n, state_m, nseq_total, steps):
    n = nseq_total * steps
    R = LANES
    spt = R // steps
    m_rows = jnp.pad(jnp.repeat(state_m.astype(F32), steps, axis=0), ((0, 0), (0, LANES - H_M)))
    n_rows = jnp.repeat(state_n.astype(F32).reshape(nseq_total, W_M), steps, axis=0)
    n4 = state_n.astype(F32).reshape(nseq_total, H_M, 1, D_HM)
    colh = lambda w: pl.BlockSpec((R, w), lambda i, h: (i, h))
    sth = lambda *s: pl.BlockSpec((spt, None) + s, lambda i, h: (i, h) + (0,) * len(s))
    sds = jax.ShapeDtypeStruct
    out, c, nn, m = pl.pallas_call(
        functools.partial(_mlstm_sample_kernel, steps=steps),
        grid=(n // R, H_M),
        in_specs=[colh(D_HM), colh(D_HM), colh(D_HM), colh(D_HM),
                  pl.BlockSpec((R, LANES), lambda i, h: (i, 0)),
                  pl.BlockSpec((1, LANES), lambda i, h: (0, 0)),
                  pl.BlockSpec((1, D_HM), lambda i, h: (0, h)),
                  pl.BlockSpec((R, LANES), lambda i, h: (i, 0)),
                  colh(D_HM), sth(D_HM, D_HM), sth(1, D_HM)],
        out_specs=[colh(D_HM), sth(D_HM, D_HM), sth(1, D_HM), sth(1, LANES)],
        out_shape=[sds((n, W_M), BF16), sds((nseq_total, H_M, D_HM, D_HM), F32),
                   sds((nseq_total, H_M, 1, D_HM), F32), sds((nseq_total, H_M, 1, LANES), F32)],
        scratch_shapes=[pltpu.VMEM((R, D_HM), F32)],
        compiler_params=_cparams(("arbitrary", "arbitrary")),
        name="mlstm_sample",
    )(pr["mq"], pr["mk"], pr["mv"], pr["mo"], pr["sm"], bias, norm_g.reshape(1, W_M).astype(F32),
      m_rows, n_rows, state_c.astype(F32), n4)
    return out, c, nn.reshape(nseq_total, H_M, D_HM), m[:, :, 0, 0]


def _rel_bucket(dist):
    max_exact = N_BUCKETS // 2
    d = jnp.maximum(dist, 0)
    df = jnp.maximum(d.astype(F32), 1.0)
    large = max_exact + (jnp.log(df / max_exact) / math.log(MAX_DISTANCE / max_exact)
                         * (N_BUCKETS - max_exact)).astype(I32)
    large = jnp.minimum(large, N_BUCKETS - 1)
    return jnp.where(d < max_exact, d, large)


def _bias_delta(rel_bias, dist):
    far = rel_bias[_rel_bucket(jnp.asarray(4 * MAX_DISTANCE, I32))]
    return (rel_bias[_rel_bucket(dist)] - far).astype(F32)


def _sortable(score):
    score = jnp.where(score == 0.0, 0.0, score)
    bits = pltpu.bitcast(score, I32)
    return bits ^ ((bits >> 31) & 0x7FFFFFFF)


def _key_value(key):
    return pltpu.bitcast(key ^ ((key >> 31) & 0x7FFFFFFF), F32)


def _bracket_search(count_ge, lo_k, c_lo, hi_k, c_hi, active, topk, iters):
    logk = math.log(topk)
    fval = lambda c: jnp.log(c + 0.5) - logk

    def cond(st):
        return jnp.logical_and(st[0] < iters, jnp.max(st[-1]) > 0.5)

    def body(st):
        it, lo_k, hi_k, f_lo, f_hi, c_lo, side, act = st
        lo_v, hi_v = _key_value(lo_k), _key_value(hi_k)
        w = jnp.clip(f_lo / (f_lo - f_hi), 1.0 / 64, 63.0 / 64)
        t_k = _sortable(lo_v + (hi_v - lo_v) * w)
        mid = (lo_k >> 1) + (hi_k >> 1) + (lo_k & hi_k & 1)
        t_k = jnp.where(jnp.logical_or(t_k <= lo_k, t_k >= hi_k), mid, t_k)
        c = count_ge(t_k)
        live = act > 0.5
        up_lo = jnp.logical_and(live, c >= topk)
        up_hi = jnp.logical_and(live, c < topk)
        fn = fval(c)
        f_hi = jnp.where(jnp.logical_and(up_lo, side > 0.5), 0.5 * f_hi, f_hi)
        f_lo = jnp.where(jnp.logical_and(up_hi, side < -0.5), 0.5 * f_lo, f_lo)
        lo_k = jnp.where(up_lo, t_k, lo_k)
        f_lo = jnp.where(up_lo, fn, f_lo)
        c_lo = jnp.where(up_lo, c, c_lo)
        hi_k = jnp.where(up_hi, t_k, hi_k)
        f_hi = jnp.where(up_hi, fn, f_hi)
        side = jnp.where(up_lo, 1.0, jnp.where(up_hi, -1.0, side))
        open_ = jnp.logical_and(c_lo != topk, hi_k > lo_k + 1)
        act = jnp.where(jnp.logical_and(live, open_), 1.0, 0.0)
        return it + 1, lo_k, hi_k, f_lo, f_hi, c_lo, side, act

    open0 = jnp.logical_and(c_lo != topk, hi_k > lo_k + 1)
    act0 = jnp.where(jnp.logical_and(active, open0), 1.0, 0.0)
    init = (jnp.int32(0), lo_k, hi_k, fval(c_lo), fval(c_hi), c_lo, jnp.zeros_like(c_lo), act0)
    st = lax.while_loop(cond, body, init)
    return st[1], st[5], st[-1] > 0.5


def _kth_largest_search(count_ge, total, short, topk, kmin=None, kmax=None):
    def cond(st):
        bit, _, _, done = st
        return jnp.logical_and(bit >= 0, jnp.min(done) < 0.5)

    def body(st):
        bit, pref, cnt, done = st
        cand_u = pref | lax.shift_left(jnp.int32(1), bit)
        c = count_ge(cand_u ^ INT_MIN)
        take = jnp.logical_and(c >= topk, done < 0.5)
        pref = jnp.where(take, cand_u, pref)
        cnt = jnp.where(take, c, cnt)
        done = jnp.where(cnt == topk, 1.0, done)
        return bit - 1, pref, cnt, done

    c_nonneg = count_ge(jnp.zeros(short.shape, I32))
    c_pos = count_ge(jnp.ones(short.shape, I32))
    nonneg = c_nonneg >= topk
    zero_thr = jnp.logical_and(nonneg, c_pos < topk)
    pref0 = jnp.where(nonneg, INT_MIN, 0)
    cnt0 = jnp.where(nonneg, c_nonneg, total)
    done0 = jnp.logical_or(jnp.logical_or(short, zero_thr), cnt0 == topk)
    if kmin is not None:
        lo_k = jnp.where(nonneg, 1, kmin)
        hi_k = jnp.where(nonneg, kmax + 1, 0)
        c_lo = jnp.where(nonneg, c_pos, total)
        c_hi = jnp.where(nonneg, 0.0, c_nonneg)
        lo_k, c_lo, left_open = _bracket_search(count_ge, lo_k, c_lo, hi_k, c_hi, jnp.logical_not(done0),
                                                topk, BRACKET_ITERS)
        found = jnp.logical_and(jnp.logical_not(done0), jnp.logical_not(left_open))
        done0 = jnp.logical_or(done0, found)
    init = (jnp.int32(30), pref0, cnt0, jnp.where(done0, 1.0, 0.0))
    _, pref, cnt, _ = lax.while_loop(cond, body, init)
    thr = pref ^ INT_MIN
    if kmin is not None:
        thr = jnp.where(found, lo_k, thr)
        cnt = jnp.where(found, c_lo, cnt)
    thr = jnp.where(short, INT_MIN + 1, thr)
    return thr, jnp.logical_and(jnp.logical_not(short), cnt != topk)


def _dsa_prompt_kernel(qs_ref, lo_ref, hi_ref, aq_ref, ikt_ref, kt_ref, v_ref, dz_ref, dzmax_ref, o_ref,
                       buf_ref, q2_ref, bnd_ref, kmax_ref, m_ref, l_ref, ls2_ref, acc2_ref, *, tq, topk):
    CH = KEY_CHUNK
    q0 = pl.program_id(1) * tq
    nchunk = jnp.maximum(lax.shift_right_logical(q0 + tq + 2 * CH - 1, 10), 1) * 2

    @pl.when(pl.program_id(1) == 0)
    def _():
        def body(c, mx):
            kc = kt_ref[:, pl.ds(pl.multiple_of(c * CH, CH), CH)].astype(F32)
            sq = kc * kc
            parts = [jnp.sum(sq[h * D_HA:(h + 1) * D_HA], axis=0, keepdims=True) for h in range(H_A)]
            return jnp.maximum(mx, jnp.concatenate(parts, axis=0))
        mx = lax.fori_loop(0, kt_ref.shape[1] // CH, body, jnp.zeros((H_A, CH), F32))
        km = jnp.sqrt(jnp.max(mx, axis=1, keepdims=True))
        for h in range(H_A):
            kmax_ref[h] = jnp.broadcast_to(km[h:h + 1, :], (1, LANES))

    qs = qs_ref[...]
    lo = lo_ref[...]
    hi = hi_ref[...]
    qs_hm = jnp.concatenate([qs[:, h * D_I:(h + 1) * D_I] for h in range(H_I)], axis=0)
    lo_hm = jnp.concatenate([lo[:, SMALL_IW + h:SMALL_IW + h + 1] for h in range(H_I)], axis=0)
    hi_hm = jnp.concatenate([hi[:, SMALL_IW + h:SMALL_IW + h + 1] for h in range(H_I)], axis=0)
    qpos = q0 + lax.broadcasted_iota(I32, (tq, CH), 0)
    lane_ch = lax.broadcasted_iota(I32, (tq, CH), 1)

    def fold(op, x):
        return functools.reduce(op, [x[:, g * LANES:(g + 1) * LANES] for g in range(x.shape[1] // LANES)])

    def score_chunk(c, carry, causal):
        kmax_p, kmin_p = carry
        st = pl.multiple_of(c * CH, CH)
        a = jnp.dot(qs_hm, ikt_ref[:, pl.ds(st, CH)], preferred_element_type=F32)
        t = jnp.maximum(jnp.minimum(a, hi_hm), lo_hm)
        sc = sum(t[h * tq:(h + 1) * tq] for h in range(H_I))
        key = _sortable(sc)
        if not causal:
            buf_ref[:, pl.ds(st, CH)] = key
            return (jnp.maximum(kmax_p, fold(jnp.maximum, key)), jnp.minimum(kmin_p, fold(jnp.minimum, key)))
        valid = st + lane_ch <= qpos
        buf_ref[:, pl.ds(st, CH)] = jnp.where(valid, key, INT_MIN)
        return (jnp.maximum(kmax_p, fold(jnp.maximum, jnp.where(valid, key, KEY_NEG_INF))),
                jnp.minimum(kmin_p, fold(jnp.minimum, jnp.where(valid, key, KEY_POS_INF))))

    n_before = lax.shift_right_logical(q0, 9)
    carry = (jnp.full((tq, LANES), KEY_NEG_INF, I32), jnp.full((tq, LANES), KEY_POS_INF, I32))
    carry = lax.fori_loop(0, n_before, functools.partial(score_chunk, causal=False), carry)
    kmax_p, kmin_p = lax.fori_loop(n_before, nchunk, functools.partial(score_chunk, causal=True), carry)
    kmax = _sortable(jnp.max(_key_value(kmax_p), axis=1, keepdims=True))
    kmin = _sortable(jnp.min(_key_value(kmin_p), axis=1, keepdims=True))

    def count_ge(cand):
        candb = jnp.broadcast_to(cand, (tq, LANES))

        def body(c, acc):
            st = c * (2 * CH)
            for g in range(2 * CH // LANES):
                blk = buf_ref[:, pl.ds(pl.multiple_of(st + g * LANES, LANES), LANES)]
                acc = acc + jnp.where(blk >= candb, 1.0, 0.0)
            return acc

        acc = lax.fori_loop(0, nchunk // 2, body, jnp.zeros((tq, LANES), F32))
        return jnp.sum(acc, axis=1, keepdims=True)

    nvalid = q0 + 1 + lax.broadcasted_iota(I32, (tq, 1), 0)
    short = nvalid <= topk
    thr, tie = _kth_largest_search(count_ge, (nchunk * CH).astype(F32), short, topk, kmin, kmax)
    any_tie = jnp.max(jnp.where(tie, 1.0, 0.0)) > 0.5

    @pl.when(jnp.logical_not(any_tie))
    def _():
        def fin(c, carry):
            st = pl.multiple_of(c * (2 * CH), 2 * CH)
            blk = buf_ref[:, pl.ds(st, 2 * CH)]
            buf_ref[:, pl.ds(st, 2 * CH)] = jnp.where(blk >= thr, 0, NEG_BITS)
            return carry
        lax.fori_loop(0, nchunk // 2, fin, 0)

    @pl.when(any_tie)
    def _():
        need = jnp.broadcast_to(topk - count_ge(thr + 1), (tq, LANES))
        thr_b = jnp.broadcast_to(thr, (tq, LANES))
        r_i = lax.broadcasted_iota(I32, (LANES, LANES), 0)
        c_i = lax.broadcasted_iota(I32, (LANES, LANES), 1)
        upper = jnp.where(r_i < c_i, 1.0, 0.0).astype(BF16)
        ones = jnp.ones((LANES, LANES), BF16)
        groups = 2 * CH // LANES

        def fin(c, seen):
            sts = [pl.multiple_of(c * (2 * CH) + g * LANES, LANES) for g in range(groups)]
            blks = [buf_ref[:, pl.ds(st, LANES)] for st in sts]
            eqs = [jnp.where(b == thr_b, 1.0, 0.0).astype(BF16) for b in blks]
            pres = [jnp.dot(e, upper, preferred_element_type=F32) for e in eqs]
            tots = [jnp.dot(e, ones, preferred_element_type=F32) for e in eqs]
            for g in range(groups):
                sel = jnp.logical_or(blks[g] > thr_b,
                                     jnp.logical_and(blks[g] == thr_b, pres[g] + seen < need))
                buf_ref[:, pl.ds(sts[g], LANES)] = jnp.where(sel, 0, NEG_BITS)
                seen = seen + tots[g]
            return seen
        lax.fori_loop(0, nchunk // 2, fin, jnp.zeros((tq, LANES), F32))

    lane = lax.broadcasted_iota(I32, (tq, LANES), 1)
    left = lane < D_HA
    for hp in range(H_A // 2):
        aqp = aq_ref[:, hp * LANES:(hp + 1) * LANES]
        zero = jnp.zeros_like(aqp)
        q2_ref[hp, 0:tq, :] = jnp.where(left, aqp, zero)
        q2_ref[hp, tq:2 * tq, :] = jnp.where(left, zero, aqp)
    ts = jnp.maximum(q0 - CH, 0)
    nfar = lax.shift_right_logical(ts + CH - 1, 9)
    lim = nfar * CH - ts
    covered = jnp.where(lax.broadcasted_iota(I32, (tq, TAIL), 1) < lim, NEG, 0.0)
    dz_start = pl.multiple_of(CH - (q0 - ts), LANES)
    two = lambda a: jnp.concatenate([a, a], axis=0)

    def logits(hp, st, width, mb2, with_dz):
        kt = kt_ref[hp * LANES:(hp + 1) * LANES, pl.ds(st, width)]
        s = jnp.dot(q2_ref[hp], kt, preferred_element_type=F32) + mb2
        if with_dz:
            s = s + jnp.concatenate([dz_ref[2 * hp, :, pl.ds(dz_start, width)],
                                     dz_ref[2 * hp + 1, :, pl.ds(dz_start, width)]], axis=0)
        return s

    def mask_bias(st, width, extra):
        mb = pltpu.bitcast(buf_ref[:, pl.ds(st, width)], F32)
        return two(mb if extra is None else mb + extra)

    def lane_groups(p):
        return sum(p[:, g * LANES:(g + 1) * LANES] for g in range(p.shape[1] // LANES))

    for hp in range(H_A // 2):
        q2 = q2_ref[hp].astype(F32)
        qn = jnp.sqrt(jnp.sum(q2 * q2, axis=1, keepdims=True))
        r2 = lax.broadcasted_iota(I32, (2 * tq, 1), 0)
        kmax = jnp.where(r2 < tq, kmax_ref[2 * hp][:, 0:1], kmax_ref[2 * hp + 1][:, 0:1])
        dzmax = jnp.where(r2 < tq, dzmax_ref[2 * hp][:, 0:1], dzmax_ref[2 * hp + 1][:, 0:1])
        bnd_ref[hp] = qn * kmax + dzmax
    acc2_ref[...] = jnp.zeros(acc2_ref.shape, F32)
    ls2_ref[...] = jnp.zeros(ls2_ref.shape, F32)

    def attend_fast(st, width, extra, with_dz):
        mb2 = mask_bias(st, width, extra)
        for hp in range(H_A // 2):
            p = jnp.exp2(logits(hp, st, width, mb2, with_dz) - bnd_ref[hp])
            ls2_ref[hp] = ls2_ref[hp] + lane_groups(p)
            vv = v_ref[pl.ds(st, width), hp * LANES:(hp + 1) * LANES]
            acc2_ref[hp] = acc2_ref[hp] + jnp.dot(p.astype(BF16), vv, preferred_element_type=F32)

    def far_fast(jc, carry):
        attend_fast(pl.multiple_of(jc * 2 * CH, 2 * CH), 2 * CH, None, False)
        return carry

    lax.fori_loop(0, nfar // 2, far_fast, 0)

    @pl.when(nfar % 2 == 1)
    def _():
        attend_fast(pl.multiple_of((nfar - 1) * CH, CH), CH, None, False)
    attend_fast(pl.multiple_of(ts, LANES), TAIL, covered, True)
    lmin = jnp.float32(jnp.inf)
    for hp in range(H_A // 2):
        l2 = jnp.sum(ls2_ref[hp], axis=1, keepdims=True)
        lmin = jnp.minimum(lmin, jnp.min(l2))
        acc = acc2_ref[hp]
        o_ref[:, hp * LANES:(hp + 1) * LANES] = jnp.where(left, acc[0:tq] / l2[0:tq], acc[tq:2 * tq] / l2[tq:2 * tq])
    underflow = jnp.logical_not(lmin > 1e-30)

    @pl.when(underflow)
    def _():
        m_ref[...] = jnp.full(m_ref.shape, NEG, F32)
        l_ref[...] = jnp.zeros(l_ref.shape, F32)
        acc2_ref[...] = jnp.zeros(acc2_ref.shape, F32)

        def attend_exact(st, width, extra, with_dz):
            mb2 = mask_bias(st, width, extra)
            for hp in range(H_A // 2):
                s = logits(hp, st, width, mb2, with_dz)
                m_old = m_ref[hp]
                m_new = jnp.maximum(m_old, jnp.max(s, axis=1, keepdims=True))
                alpha = jnp.exp2(m_old - m_new)
                p = jnp.exp2(s - m_new)
                l_ref[hp] = alpha * l_ref[hp] + jnp.sum(p, axis=1, keepdims=True)
                m_ref[hp] = m_new
                vv = v_ref[pl.ds(st, width), hp * LANES:(hp + 1) * LANES]
                acc2_ref[hp] = alpha * acc2_ref[hp] + jnp.dot(p.astype(BF16), vv, preferred_element_type=F32)

        def far_exact(jc, carry):
            attend_exact(pl.multiple_of(jc * CH, CH), CH, None, False)
            return carry

        lax.fori_loop(0, nfar, far_exact, 0)
        attend_exact(pl.multiple_of(ts, LANES), TAIL, covered, True)
        for hp in range(H_A // 2):
            acc = acc2_ref[hp]
            l2 = l_ref[hp]
            o_ref[:, hp * LANES:(hp + 1) * LANES] = jnp.where(left, acc[0:tq] / l2[0:tq],
                                                              acc[tq:2 * tq] / l2[tq:2 * tq])


def _dsa_prompt(pr, rel_bias, nb, seq, tq):
    topk = min(TOPK_MAX, seq // 4)
    assert seq % (2 * KEY_CHUNK) == 0 and seq >= TAIL + KEY_CHUNK and tq == LANES
    wtab = KEY_CHUNK + TAIL
    dmin, dmax = KEY_CHUNK - (wtab - 1), KEY_CHUNK + tq - 1
    g = (_bias_delta(rel_bias, jnp.arange(dmax, dmin - 1, -1, dtype=I32)) * LOG2E).T
    dz = jnp.stack([g[:, tq - 1 - r:tq - 1 - r + wtab] for r in range(tq)], axis=1)
    dzmax = jnp.broadcast_to(jnp.max(g, axis=1).reshape(H_A, 1, 1), (H_A, 1, LANES))
    nq = seq // tq
    rowq = lambda w: pl.BlockSpec((tq, w), lambda b, i: (b * nq + i, 0))
    perb = lambda r, c: pl.BlockSpec((None, r, c), lambda b, i: (b, 0, 0), pipeline_mode=pl.Buffered(1))
    vb = pr["avb"].reshape(nb, seq, W_A)
    return pl.pallas_call(
        functools.partial(_dsa_prompt_kernel, tq=tq, topk=topk),
        grid=(nb, nq),
        in_specs=[rowq(H_I * D_I), rowq(LANES), rowq(LANES), rowq(W_A),
                  perb(D_I, seq), perb(W_A, seq), perb(seq, W_A), _const_spec((H_A, tq, wtab)),
                  _const_spec((H_A, 1, LANES))],
        out_specs=rowq(W_A),
        out_shape=jax.ShapeDtypeStruct((nb * seq, W_A), F32),
        scratch_shapes=[pltpu.VMEM((tq, seq), I32),
                        pltpu.VMEM((H_A // 2, 2 * tq, LANES), BF16),
                        pltpu.VMEM((H_A // 2, 2 * tq, 1), F32),
                        pltpu.VMEM((H_A, 1, LANES), F32),
                        pltpu.VMEM((H_A // 2, 2 * tq, 1), F32), pltpu.VMEM((H_A // 2, 2 * tq, 1), F32),
                        pltpu.VMEM((H_A // 2, 2 * tq, LANES), F32),
                        pltpu.VMEM((H_A // 2, 2 * tq, LANES), F32)],
        compiler_params=_cparams(("arbitrary", "arbitrary")),
        name="dsa_prompt",
    )(pr["qs"], pr["lo"], pr["hi"], pr["aq"], pr["iktb"], pr["ktb"], vb, dz, dzmax)


def _dsa_sample_kernel(pt_ref, qs_ref, lo_ref, hi_ref, aq_ref, ikn_ref, akn_ref, avn_ref, dz_ref,
                       cki_ref, ck_ref, cv_ref, o_ref,
                       kib, kbuf, vbuf, buf_ref, sem_ki, sem_kv, *, steps, topk, n_pages, page, cpp):
    b = pl.program_id(0)
    nb = pl.num_programs(0)
    past = n_pages * page
    nch = n_pages // cpp
    ckeys = cpp * page
    width = past + LANES
    rows_a = H_A * steps

    def ki_copy(seq_i, slot, p):
        return pltpu.make_async_copy(cki_ref.at[pt_ref[seq_i, p]], kib.at[slot, p], sem_ki.at[slot])

    def kv_copies(seq_i, c, slot, p):
        pg = pt_ref[seq_i, c * cpp + p]
        return (pltpu.make_async_copy(ck_ref.at[pg], kbuf.at[slot, p], sem_kv.at[slot]),
                pltpu.make_async_copy(cv_ref.at[pg], vbuf.at[slot, p], sem_kv.at[slot]))

    def start_ki(seq_i, slot):
        def f(p, carry):
            ki_copy(seq_i, slot, p).start()
            return carry
        lax.fori_loop(0, n_pages, f, 0)

    n_slots = kbuf.shape[0]
    ahead = n_slots - 1
    total_chunks = nb * nch

    def start_kv(g):
        @pl.when(g < total_chunks)
        def _():
            def f(p, carry):
                for cp in kv_copies(g // nch, g % nch, g % n_slots, p):
                    cp.start()
                return carry
            lax.fori_loop(0, cpp, f, 0)

    def wait_kv(g):
        def f(p, carry):
            for cp in kv_copies(g // nch, g % nch, g % n_slots, p):
                cp.wait()
            return carry
        lax.fori_loop(0, cpp, f, 0)

    slot_b = b % 2

    @pl.when(b == 0)
    def _():
        start_ki(0, 0)
        for g0 in range(ahead):
            start_kv(jnp.int32(g0))

    def wait_ki(p, carry):
        ki_copy(b, slot_b, p).wait()
        return carry
    lax.fori_loop(0, n_pages, wait_ki, 0)

    @pl.when(b + 1 < nb)
    def _():
        start_ki(b + 1, 1 - slot_b)

    qs = qs_ref[...]
    lo = lo_ref[...]
    hi = hi_ref[...]
    qs_hm = jnp.concatenate([qs[:, h * D_I:(h + 1) * D_I] for h in range(H_I)], axis=0)
    lo_hm = jnp.concatenate([lo[:, SMALL_IW + h:SMALL_IW + h + 1] for h in range(H_I)], axis=0)
    hi_hm = jnp.concatenate([hi[:, SMALL_IW + h:SMALL_IW + h + 1] for h in range(H_I)], axis=0)

    def head_sum(a):
        t = jnp.maximum(jnp.minimum(a, hi_hm), lo_hm)
        return sum(t[h * steps:(h + 1) * steps] for h in range(H_I))

    def score_page(p, carry):
        a = jnp.dot(qs_hm, kib[slot_b, p].astype(BF16), preferred_element_type=F32)
        buf_ref[:, pl.ds(pl.multiple_of(p * page, page), page)] = _sortable(head_sum(a))
        return carry
    lax.fori_loop(0, n_pages, score_page, 0, unroll=4)

    ikn = jnp.concatenate([ikn_ref[...], jnp.zeros((LANES - steps, D_I), F32)], axis=0).astype(BF16)
    a_new = lax.dot_general(qs_hm, ikn, _NT, preferred_element_type=F32)
    t_i = lax.broadcasted_iota(I32, (steps, LANES), 0)
    j_i = lax.broadcasted_iota(I32, (steps, LANES), 1)
    buf_ref[:, past:width] = jnp.where(j_i <= t_i, _sortable(head_sum(a_new)), INT_MIN)

    def count_ge(cand):
        return jnp.sum(jnp.where(buf_ref[...] >= cand, 1.0, 0.0), axis=1, keepdims=True)

    nvalid = past + 1 + lax.broadcasted_iota(I32, (steps, 1), 0)
    short = nvalid <= topk
    keys = buf_ref[...]
    kmax = _sortable(jnp.max(_key_value(jnp.maximum(keys, KEY_NEG_INF)), axis=1, keepdims=True))
    kmin = _sortable(jnp.min(_key_value(jnp.where(keys == INT_MIN, KEY_POS_INF, keys)), axis=1, keepdims=True))
    thr, tie = _kth_largest_search(count_ge, jnp.float32(width), short, topk, kmin, kmax)
    any_tie = jnp.max(jnp.where(tie, 1.0, 0.0)) > 0.5

    @pl.when(jnp.logical_not(any_tie))
    def _():
        buf_ref[...] = jnp.where(buf_ref[...] >= thr, 0, NEG_BITS)

    @pl.when(any_tie)
    def _():
        need = jnp.broadcast_to(topk - count_ge(thr + 1), (steps, LANES))
        thr_b = jnp.broadcast_to(thr, (steps, LANES))
        r_i = lax.broadcasted_iota(I32, (LANES, LANES), 0)
        c_i = lax.broadcasted_iota(I32, (LANES, LANES), 1)
        upper = jnp.where(r_i < c_i, 1.0, 0.0).astype(BF16)
        ones = jnp.ones((LANES, LANES), BF16)
        cols = [slice(g * LANES, (g + 1) * LANES) for g in range(width // LANES)]
        blks = [buf_ref[:, cs] for cs in cols]
        eqs = [jnp.where(b == thr_b, 1.0, 0.0).astype(BF16) for b in blks]
        pres = [jnp.dot(e, upper, preferred_element_type=F32) for e in eqs]
        tots = [jnp.dot(e, ones, preferred_element_type=F32) for e in eqs]
        seen = jnp.zeros((steps, LANES), F32)
        for g, cs in enumerate(cols):
            sel = jnp.logical_or(blks[g] > thr_b, jnp.logical_and(blks[g] == thr_b, pres[g] + seen < need))
            buf_ref[:, cs] = jnp.where(sel, 0, NEG_BITS)
            seen = seen + tots[g]

    r_head = lax.broadcasted_iota(I32, (rows_a, W_A), 0) // steps
    c_head = lax.broadcasted_iota(I32, (rows_a, W_A), 1) // D_HA
    own = r_head == c_head
    aq_t = jnp.concatenate([aq_ref[...]] * H_A, axis=0)
    q_bd = jnp.where(own, aq_t, jnp.zeros_like(aq_t))
    tile_rows = lambda a: jnp.concatenate([a] * H_A, axis=0)

    def softmax_step(state, s, pv_fn):
        m_old, l_old, acc = state
        m_new = jnp.maximum(m_old, jnp.max(s, axis=1, keepdims=True))
        alpha = jnp.exp(m_old - m_new)
        p = jnp.exp(s - m_new)
        return m_new, alpha * l_old + jnp.sum(p, axis=1, keepdims=True), alpha * acc + pv_fn(p.astype(BF16))

    def chunk_step(c, state, last):
        g = b * nch + c
        slot = g % n_slots
        wait_kv(g)
        start_kv(g + ahead)

        s = jnp.concatenate([jnp.dot(q_bd, kbuf[slot, p].astype(BF16), preferred_element_type=F32)
                             for p in range(cpp)], axis=1)
        s = s + tile_rows(pltpu.bitcast(buf_ref[:, pl.ds(pl.multiple_of(c * ckeys, ckeys), ckeys)], F32))
        if last:
            s = s + jnp.concatenate([jnp.zeros((rows_a, ckeys - page), F32), dz_ref[:, 0:page]], axis=1)

        def pv(pb):
            return sum(lax.dot_general(pb[:, p * page:(p + 1) * page], vbuf[slot, p].astype(BF16), _NT,
                                       preferred_element_type=F32) for p in range(cpp))
        return softmax_step(state, s, pv)

    state = (jnp.full((rows_a, 1), NEG, F32), jnp.zeros((rows_a, 1), F32), jnp.zeros((rows_a, W_A), F32))
    state = lax.fori_loop(0, nch - 1, lambda c, st: chunk_step(c, st, False), state)
    state = chunk_step(nch - 1, state, True)
    pad = jnp.zeros((LANES - steps, W_A), F32)
    kn = jnp.concatenate([akn_ref[...], pad], axis=0).astype(BF16)
    vn = jnp.concatenate([avn_ref[...], pad], axis=0).astype(BF16)
    s_new = (lax.dot_general(q_bd, kn, _NT, preferred_element_type=F32)
             + tile_rows(pltpu.bitcast(buf_ref[:, past:width], F32)) + dz_ref[:, page:2 * page])
    _, l_fin, acc = softmax_step(state, s_new, lambda pb: jnp.dot(pb, vn, preferred_element_type=F32))
    accn = jnp.where(own, acc / l_fin, 0.0)
    o_ref[...] = sum(accn[h * steps:(h + 1) * steps] for h in range(H_A))


def _dsa_sample(pr, cache_k, cache_v, cache_k_idx, page_table, rel_bias, nseq, steps):
    n_pool, page = cache_k.shape[0], cache_k.shape[1]
    n_pages = page_table.shape[1]
    past = n_pages * page
    topk = min(TOPK_MAX, (past + steps) // 4)
    cpp = 8 if n_pages % 8 == 0 else n_pages
    assert page == LANES and steps <= LANES
    t = jnp.arange(steps, dtype=I32)[:, None]
    c = jnp.arange(LANES, dtype=I32)[None, :]
    dz = jnp.transpose(_bias_delta(rel_bias, jnp.concatenate([t + LANES - c, t - c], axis=1)), (2, 0, 1))
    dz = dz.reshape(H_A * steps, 2 * LANES)
    ck_t = jnp.transpose(cache_k, (0, 2, 3, 1)).reshape(n_pool, W_A, page)
    cv_t = jnp.transpose(cache_v, (0, 2, 3, 1)).reshape(n_pool, W_A, page)
    cki_t = jnp.transpose(cache_k_idx, (0, 2, 1))
    rows = lambda w: pl.BlockSpec((steps, w), lambda b, pt: (b, 0))
    anyspec = pl.BlockSpec(memory_space=pl.ANY)
    return pl.pallas_call(
        functools.partial(_dsa_sample_kernel, steps=steps, topk=topk, n_pages=n_pages, page=page, cpp=cpp),
        grid_spec=pltpu.PrefetchScalarGridSpec(
            num_scalar_prefetch=1,
            grid=(nseq,),
            in_specs=[rows(H_I * D_I), rows(LANES), rows(LANES), rows(W_A), rows(D_I), rows(W_A), rows(W_A),
                      pl.BlockSpec((H_A * steps, 2 * LANES), lambda b, pt: (0, 0)),
                      anyspec, anyspec, anyspec],
            out_specs=rows(W_A),
            scratch_shapes=[pltpu.VMEM((2, n_pages, D_I, page), F32),
                            pltpu.VMEM((KV_SLOTS, cpp, W_A, page), F32), pltpu.VMEM((KV_SLOTS, cpp, W_A, page), F32),
                            pltpu.VMEM((steps, past + LANES), I32),
                            pltpu.SemaphoreType.DMA((2,)), pltpu.SemaphoreType.DMA((KV_SLOTS,))],
        ),
        out_shape=jax.ShapeDtypeStruct((nseq * steps, W_A), F32),
        compiler_params=_cparams(("arbitrary",)),
        name="dsa_sample",
    )(page_table.astype(I32), pr["qs"], pr["lo"], pr["hi"], pr["aq"], pr["ik"], pr["ak"], pr["av"], dz,
      cki_t, ck_t, cv_t)


def _layer_norm(y, g, b):
    mu = jnp.mean(y, axis=1, keepdims=True)
    var = jnp.mean(jnp.square(y - mu), axis=1, keepdims=True)
    return (y - mu) * lax.rsqrt(var + LN_EPS) * g + b


def _mix_router_kernel(x_ref, mo_ref, ao_ref, wo1_ref, wo2_ref, g_ref, b_ref, rw_ref, rb_ref,
                       h_ref, route_ref, gate_ref, cnt_ref, carry_s):
    @pl.when(pl.program_id(0) == 0)
    def _():
        carry_s[...] = jnp.zeros_like(carry_s)

    tm = x_ref.shape[0]
    mix = (jnp.dot(mo_ref[...], wo1_ref[...], preferred_element_type=F32)
           + jnp.dot(ao_ref[...].astype(BF16), wo2_ref[...], preferred_element_type=F32))
    h = _layer_norm(DN_ALPHA * x_ref[...] + mix, g_ref[...], b_ref[...])
    h_ref[...] = h
    logits = jnp.dot(h.astype(BF16), rw_ref[...], preferred_element_type=F32) + rb_ref[...]
    lane = lax.broadcasted_iota(I32, (tm, LANES), 1)
    lane_f = lane.astype(F32)
    l = jnp.where(lane < N_EXPERTS, logits, -jnp.inf)
    vals, hots, idxs = [], [], []
    member = jnp.zeros((tm, LANES), F32)
    for _ in range(TOP_K_EXP):
        mx = jnp.max(l, axis=1, keepdims=True)
        idx = jnp.min(jnp.where(l == mx, lane_f, float(LANES)), axis=1, keepdims=True)
        hot = lane_f == idx
        vals.append(mx)
        idxs.append(idx)
        hots.append(hot)
        member = member + jnp.where(hot, 1.0, 0.0)
        l = jnp.where(hot, -jnp.inf, l)
    ex = [jnp.exp(v - vals[0]) for v in vals]
    tot = sum(ex)
    stril = (lax.broadcasted_iota(I32, (tm, tm), 1) < lax.broadcasted_iota(I32, (tm, tm), 0))
    before = jnp.dot(jnp.where(stril, 1.0, 0.0).astype(BF16), member.astype(BF16),
                     preferred_element_type=F32) + carry_s[...]
    route = jnp.zeros((tm, LANES), F32)
    gates = jnp.zeros((tm, LANES), F32)
    for k in range(TOP_K_EXP):
        rank = jnp.sum(jnp.where(hots[k], before, 0.0), axis=1, keepdims=True)
        route = route + jnp.where(lane == k, idxs[k], 0.0) + jnp.where(lane == TOP_K_EXP + k, rank, 0.0)
        gates = gates + jnp.where(lane == k, ex[k] / tot, 0.0)
    route_ref[...] = route.astype(I32)
    gate_ref[...] = gates
    carry_s[...] = carry_s[...] + jnp.sum(member, axis=0, keepdims=True)
    cnt_ref[...] = carry_s[...]


def _scatter_rows_kernel(zstart_ref, dest_ref, h_ref, xs_out, zbuf, sem, zsem):
    ts = h_ref.shape[0]

    @pl.when(pl.program_id(0) == 0)
    def _():
        zbuf[...] = jnp.zeros_like(zbuf)

        def zero_block(j, go):
            @pl.when(zstart_ref[j] >= 0)
            def _():
                rows = pl.ds(pl.multiple_of(zstart_ref[j], ROW_BLOCK), ROW_BLOCK)
                go(pltpu.make_async_copy(zbuf, xs_out.at[rows], zsem))
        for j in range(2 * N_EXPERTS):
            zero_block(j, lambda cp: cp.start())
        for j in range(2 * N_EXPERTS):
            zero_block(j, lambda cp: cp.wait())

    def start(t, carry):
        for k in range(TOP_K_EXP):
            d = dest_ref[t * TOP_K_EXP + k]
            pltpu.make_async_copy(h_ref.at[pl.ds(t, 1)], xs_out.at[pl.ds(d, 1)], sem).start(priority=k % 2)
        return carry
    lax.fori_loop(0, ts, start, 0)

    def wait(t, carry):
        for k in range(TOP_K_EXP):
            pltpu.make_async_copy(h_ref.at[pl.ds(0, 1)], xs_out.at[pl.ds(0, 1)], sem).wait()
        return carry
    lax.fori_loop(0, ts, wait, 0)


def _expert_kernel(be_ref, na_ref, xs_ref, wgu_ref, bgu_ref, wdn_ref, bdn_ref, ys_ref):
    del be_ref
    active = pl.program_id(0) < na_ref[0]

    @pl.when(jnp.logical_not(active))
    def _():
        ys_ref[...] = jnp.zeros_like(ys_ref)

    @pl.when(active)
    def _():
        f = wdn_ref.shape[0]
        gu = jnp.dot(xs_ref[...].astype(BF16), wgu_ref[...], preferred_element_type=F32) + bgu_ref[...]
        gt = jnp.minimum(gu[:, :f], SWIGLU_LIMIT)
        up = jnp.clip(gu[:, f:], -SWIGLU_LIMIT, SWIGLU_LIMIT)
        act = (up + 1.0) * gt / (1.0 + jnp.exp(-SWIGLU_ALPHA * gt))
        ys_ref[...] = jnp.dot(act.astype(BF16), wdn_ref[...], preferred_element_type=F32) + bdn_ref[...]


def _combine_kernel(dest_ref, dest_next_ref, gate_ref, h_ref, g_ref, b_ref, ys_ref, o_ref, ybuf, sem):
    tc = h_ref.shape[0]
    i = pl.program_id(0)
    slot = i % 2

    def copy(s, t, k, d):
        return pltpu.make_async_copy(ys_ref.at[pl.ds(d, 1)], ybuf.at[s, k, pl.ds(t, 1)], sem.at[s])

    def start_tile(idx_ref, s):
        def start(t, carry):
            for k in range(TOP_K_EXP):
                copy(s, t, k, idx_ref[t * TOP_K_EXP + k]).start(priority=k % 2)
            return carry
        lax.fori_loop(0, tc, start, 0)

    @pl.when(i == 0)
    def _():
        start_tile(dest_ref, 0)

    @pl.when(i + 1 < pl.num_programs(0))
    def _():
        start_tile(dest_next_ref, 1 - slot)

    def wait(t, carry):
        for k in range(TOP_K_EXP):
            copy(slot, 0, k, 0).wait()
        return carry
    lax.fori_loop(0, tc, wait, 0)
    gates = gate_ref[...]
    y = sum(gates[:, k:k + 1] * ybuf[slot, k] for k in range(TOP_K_EXP))
    o_ref[...] = _layer_norm(DN_ALPHA * h_ref[...] + y, g_ref[...], b_ref[...])


def _finish(x2, m_out, a_out, ffn, tm, tr):
    n, d = x2.shape
    wo1, wo2, ln1_g, ln1_b, rw, rb, wgu, bgu, wdn, bdn, ln2_g, ln2_b = ffn
    f = wdn.shape[1]
    vec = lambda a: a.reshape(1, -1).astype(F32)
    row = lambda t, w: pl.BlockSpec((t, w), lambda i, *_: (i, 0))
    sds = jax.ShapeDtypeStruct
    h, route, gates, cnt = pl.pallas_call(
        _mix_router_kernel,
        grid=(n // tm,),
        in_specs=[row(tm, d), row(tm, W_M), row(tm, W_A), _const_spec(wo1.shape), _const_spec(wo2.shape),
                  _const_spec((1, d)), _const_spec((1, d)), _const_spec(rw.shape), _const_spec((1, LANES))],
        out_specs=[row(tm, d), row(tm, LANES), row(tm, LANES), pl.BlockSpec((1, LANES), lambda i: (0, 0))],
        out_shape=[sds((n, d), F32), sds((n, LANES), I32), sds((n, LANES), F32), sds((1, LANES), F32)],
        scratch_shapes=[pltpu.VMEM((1, LANES), F32)],
        compiler_params=_cparams(("arbitrary",)),
        name="mix_router",
    )(x2, m_out, a_out, wo1, wo2, vec(ln1_g), vec(ln1_b), rw, rb)

    counts = cnt[0, :N_EXPERTS].astype(I32)
    padded = (counts + ROW_BLOCK - 1) // ROW_BLOCK * ROW_BLOCK
    pad_end = jnp.cumsum(padded)
    pad_start = pad_end - padded
    dest = (pad_start[route[:, :TOP_K_EXP]] + route[:, TOP_K_EXP:2 * TOP_K_EXP]).reshape(-1)
    n_blk = -(-n * TOP_K_EXP // ROW_BLOCK) + N_EXPERTS
    blk_start = jnp.arange(n_blk, dtype=I32) * ROW_BLOCK
    blk_e = jnp.minimum(jnp.sum((pad_end[None, :] <= blk_start[:, None]).astype(I32), axis=1), N_EXPERTS - 1)
    n_act = (pad_end[-1:] // ROW_BLOCK).astype(I32)
    tail_rows = pad_end[-1] + jnp.arange(N_EXPERTS, dtype=I32) * ROW_BLOCK
    zero_blocks = jnp.concatenate([jnp.where(counts > 0, pad_end - ROW_BLOCK, -1),
                                   jnp.where(tail_rows < n_blk * ROW_BLOCK, tail_rows, -1)]).astype(I32)

    n_steps = n // tr
    smem_dest = pl.BlockSpec((tr * TOP_K_EXP,), lambda i, *_: (i,), memory_space=pltpu.SMEM)
    smem_next = pl.BlockSpec((tr * TOP_K_EXP,), lambda i, *_: (jnp.minimum(i + 1, n_steps - 1),),
                             memory_space=pltpu.SMEM)
    anyspec = pl.BlockSpec(memory_space=pl.ANY)
    xs = pl.pallas_call(
        _scatter_rows_kernel,
        grid_spec=pltpu.PrefetchScalarGridSpec(
            num_scalar_prefetch=1,
            grid=(n_steps,),
            in_specs=[smem_dest, row(tr, d)],
            out_specs=anyspec,
            scratch_shapes=[pltpu.VMEM((ROW_BLOCK, d), F32), pltpu.SemaphoreType.DMA(()),
                            pltpu.SemaphoreType.DMA(())],
        ),
        out_shape=sds((n_blk * ROW_BLOCK, d), F32),
        compiler_params=_cparams(("arbitrary",)),
        name="moe_scatter",
    )(zero_blocks, dest, h)

    blk = lambda i, be, na: jnp.minimum(i, na[0] - 1)
    ys = pl.pallas_call(
        _expert_kernel,
        grid_spec=pltpu.PrefetchScalarGridSpec(
            num_scalar_prefetch=2,
            grid=(n_blk,),
            in_specs=[pl.BlockSpec((ROW_BLOCK, d), lambda i, be, na: (blk(i, be, na), 0)),
                      pl.BlockSpec((None, d, 2 * f), lambda i, be, na: (be[i], 0, 0)),
                      pl.BlockSpec((None, 1, 2 * f), lambda i, be, na: (be[i], 0, 0)),
                      pl.BlockSpec((None, f, d), lambda i, be, na: (be[i], 0, 0)),
                      pl.BlockSpec((None, 1, d), lambda i, be, na: (be[i], 0, 0))],
            out_specs=pl.BlockSpec((ROW_BLOCK, d), lambda i, be, na: (i, 0)),
        ),
        out_shape=sds((n_blk * ROW_BLOCK, d), F32),
        compiler_params=_cparams(("arbitrary",)),
        name="moe_experts",
    )(blk_e, n_act, xs, wgu, bgu, wdn, bdn)

    return pl.pallas_call(
        _combine_kernel,
        grid=(n_steps,),
        in_specs=[smem_dest, smem_next, row(tr, LANES), row(tr, d), _const_spec((1, d)), _const_spec((1, d)),
                  anyspec],
        out_specs=row(tr, d),
        out_shape=sds((n, d), F32),
        scratch_shapes=[pltpu.VMEM((2, TOP_K_EXP, tr, d), F32), pltpu.SemaphoreType.DMA((2,))],
        compiler_params=_cparams(("arbitrary",)),
        name="moe_combine",
    )(dest, dest, gates, h, vec(ln2_g), vec(ln2_b), ys)


def _pack_ffn(w_out, ln1_g, ln1_b, router_w, router_b, w_gu, b_gu, w_dn, b_dn, ln2_g, ln2_b):
    d = w_out.shape[1]
    rw = jnp.pad(router_w, ((0, 0), (0, LANES - N_EXPERTS))).astype(BF16)
    rb = jnp.pad(router_b.astype(F32), (0, LANES - N_EXPERTS)).reshape(1, LANES)
    return (w_out[:W_M].astype(BF16), w_out[W_M:].astype(BF16), ln1_g, ln1_b, rw, rb,
            w_gu.astype(BF16), b_gu.astype(F32).reshape(N_EXPERTS, 1, -1),
            w_dn.astype(BF16), b_dn.astype(F32).reshape(N_EXPERTS, 1, d), ln2_g, ln2_b)


def _tile(n, pref):
    t = pref
    while n % t:
        t //= 2
    return t


def kernel(x_prompt, x_sample, cache_k, cache_v, cache_k_idx, state_C, state_n, state_m, page_table,
           w_in, b_ig, b_fg, mlstm_norm_g, w_out, ln1_g, ln1_b, router_w, router_b,
           w_gate_up, b_gate_up, w_down, b_down, ln2_g, ln2_b, rel_bias):
    assert w_in.shape[0] == DEPTH
    nb, seq, d = x_prompt.shape
    ns, steps, _ = x_sample.shape
    bias = _gate_bias(b_ig[0], b_fg[0])
    ffn = _pack_ffn(w_out[0], ln1_g[0], ln1_b[0], router_w[0], router_b[0], w_gate_up[0], b_gate_up[0],
                    w_down[0], b_down[0], ln2_g[0], ln2_b[0])

    xp = x_prompt.reshape(nb * seq, d)
    pp = _project(xp, w_in[0], nb, seq, _tile(seq, 256), True)
    mo_p, c_p, n_p, m_p = _mlstm_prompt(pp, bias, mlstm_norm_g[0], nb, seq, LANES)
    ao_p = _dsa_prompt(pp, rel_bias, nb, seq, LANES)
    y_p = _finish(xp, mo_p, ao_p, ffn, _tile(nb * seq, 512), _tile(nb * seq, 256))

    xs = x_sample.reshape(ns * steps, d)
    ps = _project(xs, w_in[0], 1, ns * steps, _tile(ns * steps, 256), False)
    mo_s, c_s, n_s, m_s = _mlstm_sample(ps, bias, mlstm_norm_g[0], state_C[0], state_n[0], state_m[0], ns, steps)
    ao_s = _dsa_sample(ps, cache_k[0], cache_v[0], cache_k_idx[0], page_table, rel_bias, ns, steps)
    y_s = _finish(xs, mo_s, ao_s, ffn, _tile(ns * steps, 512), _tile(ns * steps, 256))

    dt = x_prompt.dtype
    L = lambda a: a[None]
    heads = lambda a: jnp.transpose(a.reshape(nb, H_A, D_HA, seq), (0, 3, 1, 2))
    return (y_p.reshape(nb, seq, d), y_s.reshape(ns, steps, d),
            L(heads(pp["ktf"])), L(heads(pp["vtf"])), L(jnp.transpose(pp["iktf"], (0, 2, 1))),
            L(c_p.astype(dt)), L(n_p.astype(dt)), L(m_p.astype(dt)),
            L(ps["ak"].reshape(ns, steps, H_A, D_HA)), L(ps["av"].reshape(ns, steps, H_A, D_HA)),
            L(ps["ik"].reshape(ns, steps, D_I)),
            L(c_s.astype(state_C.dtype)), L(n_s.astype(state_n.dtype)), L(m_s.astype(state_m.dtype)))
```

```python
import functools
import math

import numpy as np
import jax
import jax.numpy as jnp
from jax import lax
from jax.experimental import pallas as pl
from jax.experimental.pallas import tpu as pltpu

F32, BF16, I32 = jnp.float32, jnp.bfloat16, jnp.int32

H_M, D_HM = 4, 128
H_A, D_HA = 8, 64
H_I, D_I = 4, 64
W_M, W_A = H_M * D_HM, H_A * D_HA
TOPK_MAX = 256
N_BUCKETS, MAX_DISTANCE = 32, 128
N_EXPERTS, TOP_K_EXP = 32, 4
SWIGLU_LIMIT, SWIGLU_ALPHA = 7.0, 1.702
GATE_CAP = 15.0
LN_EPS = 1e-5
DEPTH = 1
DN_ALPHA = (2.0 * DEPTH) ** 0.25

LANES = 128
VMEM_LIMIT_BYTES = 56 * 1024 * 1024
LOG2E = 1.4426950408889634
NEG = -1e30
NEG_BITS = int(np.float32(NEG).view(np.int32))
INT_MIN = -2 ** 31
KEY_POS_INF = 0x7F800000
KEY_NEG_INF = (0xFF800000 ^ 0x7FFFFFFF) - 2 ** 32
KEY_CHUNK = 512
TAIL = KEY_CHUNK + LANES
ROW_BLOCK = 512
BRACKET_ITERS = 24
KV_SLOTS = 4
SMALL_IK, SMALL_MI, SMALL_MF, SMALL_IW = 0, 64, 68, 72

_NT = (((1,), (1,)), ((), ()))


def _cparams(sem):
    return pltpu.CompilerParams(dimension_semantics=sem, vmem_limit_bytes=VMEM_LIMIT_BYTES)


def _const_spec(shape):
    zeros = (0,) * len(shape)
    return pl.BlockSpec(shape, lambda *_: zeros, pipeline_mode=pl.Buffered(1))


_PROJ_COMMON = ("mq", "mk", "mv", "mo", "sm", "aq", "qs", "lo", "hi")
_PROJ_EXTRA = {
    True: ("avb", "ktb", "iktb", "ktf", "vtf", "iktf"),
    False: ("ak", "av", "ik"),
}


def _proj_kernel(x_ref, wm_ref, wa_ref, wi_ref, ws_ref, wt_ref, *out_refs, transposed):
    o = dict(zip(_PROJ_COMMON + _PROJ_EXTRA[transposed], out_refs))
    xb = x_ref[...].astype(BF16)
    ym = jnp.dot(xb, wm_ref[...], preferred_element_type=F32)
    o["mq"][...] = ym[:, 0:W_M].astype(BF16)
    o["mk"][...] = (ym[:, W_M:2 * W_M] * (D_HM ** -0.5)).astype(BF16)
    o["mv"][...] = ym[:, 2 * W_M:3 * W_M].astype(BF16)
    o["mo"][...] = ym[:, 3 * W_M:4 * W_M]
    ya = jnp.dot(xb, wa_ref[...], preferred_element_type=F32)
    o["aq"][...] = (ya[:, 0:W_A] * (D_HA ** -0.5 * (LOG2E if transposed else 1.0))).astype(BF16)
    xl = (x_ref[...] - xb.astype(F32)).astype(BF16)
    ys = (jnp.dot(xb, ws_ref[0], preferred_element_type=F32) + jnp.dot(xl, ws_ref[0], preferred_element_type=F32)
          + jnp.dot(xb, ws_ref[1], preferred_element_type=F32))
    o["sm"][...] = ys
    yi = jnp.dot(xb, wi_ref[...], preferred_element_type=F32)
    wsc = ys * (H_I ** -0.5 * D_I ** -0.5)
    lane = lax.broadcasted_iota(I32, yi.shape, 1)
    scale = wsc[:, SMALL_IW + 3:SMALL_IW + 4]
    for h in (2, 1, 0):
        scale = jnp.where(lane < (h + 1) * D_I, wsc[:, SMALL_IW + h:SMALL_IW + h + 1], scale)
    o["qs"][...] = (yi * scale).astype(BF16)
    o["lo"][...] = jnp.where(ys >= 0.0, 0.0, -jnp.inf)
    o["hi"][...] = jnp.where(ys <= 0.0, 0.0, jnp.inf)
    if transposed:
        o["avb"][...] = ya[:, W_A:2 * W_A].astype(BF16)
        yt = lax.dot_general(wt_ref[...], xb, _NT, preferred_element_type=F32)
        o["ktf"][...] = yt[0:W_A]
        o["ktb"][...] = yt[0:W_A].astype(BF16)
        o["vtf"][...] = yt[W_A:2 * W_A]
        o["iktf"][...] = yt[2 * W_A:2 * W_A + D_I]
        o["iktb"][...] = yt[2 * W_A:2 * W_A + D_I].astype(BF16)
    else:
        o["ak"][...] = ya[:, W_A:2 * W_A]
        o["av"][...] = ya[:, 2 * W_A:3 * W_A]
        o["ik"][...] = ys[:, SMALL_IK:SMALL_IK + D_I]


def _pack_w_in(w_in, transposed):
    d = w_in.shape[0]
    c = np.cumsum([0, W_M, W_M, W_M, W_M, H_M, H_M, W_A, W_A, W_A, H_I * D_I, D_I, H_I])
    seg = lambda i: w_in[:, c[i]:c[i + 1]]
    wm = w_in[:, c[0]:c[4]]
    wa = jnp.concatenate([seg(6), seg(8)], axis=1) if transposed else w_in[:, c[6]:c[9]]
    wi = seg(9)
    ws = jnp.concatenate([seg(10), seg(4), seg(5), seg(11),
                          jnp.zeros((d, LANES - D_I - 2 * H_M - H_I), w_in.dtype)], axis=1)
    wt = jnp.concatenate([seg(7), seg(8), seg(10)], axis=1).T
    ws_hi = ws.astype(BF16)
    ws2 = jnp.stack([ws_hi, (ws - ws_hi.astype(F32)).astype(BF16)])
    return (wm.astype(BF16), wa.astype(BF16), wi.astype(BF16), ws2, wt.astype(BF16))


def _project(x2, w_in, nb, seq, tm, transposed):
    n, d = x2.shape
    packs = _pack_w_in(w_in, transposed)
    nps = seq // tm
    sds = jax.ShapeDtypeStruct
    row = lambda w, dt: (sds((n, w), dt), pl.BlockSpec((tm, w), lambda i: (i, 0)))
    tr = lambda w, dt: (sds((nb, w, seq), dt), pl.BlockSpec((None, w, tm), lambda i: (i // nps, 0, i % nps)))
    outs = dict(
        mq=row(W_M, BF16), mk=row(W_M, BF16), mv=row(W_M, BF16), mo=row(W_M, F32), sm=row(LANES, F32),
        aq=row(W_A, BF16), qs=row(H_I * D_I, BF16), lo=row(LANES, F32), hi=row(LANES, F32),
        avb=row(W_A, BF16), ktb=tr(W_A, BF16), iktb=tr(D_I, BF16), ktf=tr(W_A, F32), vtf=tr(W_A, F32),
        iktf=tr(D_I, F32), ak=row(W_A, F32), av=row(W_A, F32), ik=row(D_I, F32),
    )
    names = _PROJ_COMMON + _PROJ_EXTRA[transposed]
    res = pl.pallas_call(
        functools.partial(_proj_kernel, transposed=transposed),
        grid=(n // tm,),
        in_specs=[pl.BlockSpec((tm, d), lambda i: (i, 0))] + [_const_spec(w.shape) for w in packs],
        out_specs=[outs[k][1] for k in names],
        out_shape=[outs[k][0] for k in names],
        compiler_params=_cparams(("arbitrary",)),
        name="proj_t" if transposed else "proj",
    )(x2, *packs)
    return dict(zip(names, res))


def _gates(sm, bias):
    cap = GATE_CAP * jnp.tanh((sm + bias) / GATE_CAP)
    e = jnp.exp(-jnp.abs(cap))
    u = 1.0 + e
    log1p_e = jnp.where(u == 1.0, e, jnp.log(u) * (e / (u - 1.0)))
    logsig = jnp.minimum(cap, 0.0) - log1p_e
    lane = lax.broadcasted_iota(I32, sm.shape, 1)
    return jnp.where(lane >= SMALL_MF, logsig, cap)


def _split3(x):
    a = x.astype(BF16)
    r = x - a.astype(F32)
    b = r.astype(BF16)
    c = (r - b.astype(F32)).astype(BF16)
    return a, b, c


def _dot3(mat, parts):
    return sum(jnp.dot(mat, p, preferred_element_type=F32) for p in parts)


def _headnorm_out(hh, mo, g):
    mu = jnp.mean(hh, axis=1, keepdims=True)
    var = jnp.mean(jnp.square(hh - mu), axis=1, keepdims=True)
    hn = (hh - mu) * lax.rsqrt(var + LN_EPS) * g
    return (hn / (1.0 + jnp.exp(-mo))).astype(BF16)


def _mlstm_prompt_kernel(q_ref, k_ref, v_ref, mo_ref, sm_ref, bias_ref, g_ref,
                         out_ref, c_out, n_out, m_out, c_s, n_s, m_s, *, chunk):
    j = pl.program_id(1)

    @pl.when(j == 0)
    def _():
        c_s[...] = jnp.zeros_like(c_s)
        n_s[...] = jnp.zeros_like(n_s)
        m_s[...] = jnp.zeros_like(m_s)

    L = chunk
    gate = _gates(sm_ref[...], bias_ref[...])
    r_i = lax.broadcasted_iota(I32, (L, L), 0)
    c_i = lax.broadcasted_iota(I32, (L, L), 1)
    tril = c_i <= r_i
    cum = _dot3(jnp.where(tril, 1.0, 0.0).astype(BF16), _split3(gate))
    gate_t = gate.T
    cum_t = cum.T
    for h in range(H_M):
        sl = slice(h * D_HM, (h + 1) * D_HM)
        b_col = cum[:, SMALL_MF + h:SMALL_MF + h + 1]
        b_row = cum_t[SMALL_MF + h:SMALL_MF + h + 1, :]
        i_col = gate[:, SMALL_MI + h:SMALL_MI + h + 1]
        i_row = gate_t[SMALL_MI + h:SMALL_MI + h + 1, :]
        m_prev = m_s[h][:, 0:1]
        dmat = jnp.where(tril, b_col - b_row + i_row, -jnp.inf)
        inter = b_col + m_prev
        mt = jnp.maximum(inter, jnp.max(dmat, axis=1, keepdims=True))
        w_intra = jnp.exp(dmat - mt)
        w_inter = jnp.exp(inter - mt)
        q, k, v = q_ref[:, sl], k_ref[:, sl], v_ref[:, sl]
        s = lax.dot_general(q, k, _NT, preferred_element_type=F32) * w_intra
        c_old = c_s[h]
        n_old = n_s[h]
        num = (jnp.dot(s.astype(BF16), v, preferred_element_type=F32)
               + w_inter * lax.dot_general(q, c_old.astype(BF16), _NT, preferred_element_type=F32))
        den = (jnp.sum(s, axis=1, keepdims=True)
               + w_inter * jnp.sum(q.astype(F32) * n_old, axis=1, keepdims=True))
        hh = num / jnp.maximum(jnp.abs(den), jnp.exp(-mt))
        out_ref[:, sl] = _headnorm_out(hh, mo_ref[:, sl], g_ref[:, sl])
        bl = b_row[:, L - 1:L]
        decay = bl - b_col + i_col
        m_new = jnp.maximum(bl + m_prev, jnp.max(decay, axis=0, keepdims=True))
        ws = jnp.exp(decay - m_new)
        cs = jnp.exp(bl + m_prev - m_new)
        vw_t = (v.astype(F32) * ws).T.astype(BF16)
        c_s[h] = cs * c_old + jnp.dot(vw_t, k, preferred_element_type=F32)
        n_s[h] = cs * n_old + jnp.sum(k.astype(F32) * ws, axis=0, keepdims=True)
        m_s[h] = jnp.broadcast_to(m_new, (1, LANES))

    @pl.when(j == pl.num_programs(1) - 1)
    def _():
        c_out[...] = c_s[...]
        n_out[...] = n_s[...]
        m_out[...] = m_s[...]


def _gate_bias(b_ig, b_fg):
    z = jnp.zeros((LANES,), F32)
    z = z.at[SMALL_MI:SMALL_MI + H_M].set(b_ig.astype(F32)).at[SMALL_MF:SMALL_MF + H_M].set(b_fg.astype(F32))
    return z.reshape(1, LANES)


def _mlstm_prompt(pr, bias, norm_g, nb, seq, chunk):
    n = nb * seq
    nc = seq // chunk
    row = lambda w: pl.BlockSpec((chunk, w), lambda b, j: (b * nc + j, 0))
    st = lambda *s: pl.BlockSpec((None,) + s, lambda b, j: (b,) + (0,) * len(s))
    sds = jax.ShapeDtypeStruct
    out, c, nn, m = pl.pallas_call(
        functools.partial(_mlstm_prompt_kernel, chunk=chunk),
        grid=(nb, nc),
        in_specs=[row(W_M), row(W_M), row(W_M), row(W_M), row(LANES),
                  pl.BlockSpec((1, LANES), lambda b, j: (0, 0)), pl.BlockSpec((1, W_M), lambda b, j: (0, 0))],
        out_specs=[row(W_M), st(H_M, D_HM, D_HM), st(H_M, 1, D_HM), st(H_M, 1, LANES)],
        out_shape=[sds((n, W_M), BF16), sds((nb, H_M, D_HM, D_HM), F32),
                   sds((nb, H_M, 1, D_HM), F32), sds((nb, H_M, 1, LANES), F32)],
        scratch_shapes=[pltpu.VMEM((H_M, D_HM, D_HM), F32), pltpu.VMEM((H_M, 1, D_HM), F32),
                        pltpu.VMEM((H_M, 1, LANES), F32)],
        compiler_params=_cparams(("arbitrary", "arbitrary")),
        name="mlstm_prompt",
    )(pr["mq"], pr["mk"], pr["mv"], pr["mo"], pr["sm"], bias, norm_g.reshape(1, W_M).astype(F32))
    return out, c, nn.reshape(nb, H_M, D_HM), m[:, :, 0, 0]


def _mlstm_sample_kernel(q_ref, k_ref, v_ref, mo_ref, sm_ref, bias_ref, g_ref, mrow_ref, nrow_ref,
                         c_in, n_in, out_ref, c_out, n_out, m_out, numi_s, *, steps):
    h = pl.program_id(1)
    R = q_ref.shape[0]
    nseq = R // steps
    gate = _gates(sm_ref[...], bias_ref[...])
    r_i = lax.broadcasted_iota(I32, (R, R), 0)
    c_i = lax.broadcasted_iota(I32, (R, R), 1)
    same = (r_i // steps) == (c_i // steps)
    mask = jnp.logical_and(same, c_i <= r_i)
    parts = _split3(gate)
    cum = _dot3(jnp.where(mask, 1.0, 0.0).astype(BF16), parts)
    tot = _dot3(jnp.where(same, 1.0, 0.0).astype(BF16), parts)
    lane = lax.broadcasted_iota(I32, (R, LANES), 1)
    sub = lax.broadcasted_iota(I32, (LANES, R), 0)
    col = lambda a, off: jnp.sum(jnp.where(lane == off + h, a, 0.0), axis=1, keepdims=True)
    rowv = lambda a, off: jnp.sum(jnp.where(sub == off + h, a.T, 0.0), axis=0, keepdims=True)
    b_col, b_row = col(cum, SMALL_MF), rowv(cum, SMALL_MF)
    i_col, i_row = col(gate, SMALL_MI), rowv(gate, SMALL_MI)
    bl_col, bl_row = col(tot, SMALL_MF), rowv(tot, SMALL_MF)
    m_col = jnp.sum(jnp.where(lane == h, mrow_ref[...], 0.0), axis=1, keepdims=True)
    dmat = jnp.where(mask, b_col - b_row + i_row, -jnp.inf)
    inter = b_col + m_col
    mt = jnp.maximum(inter, jnp.max(dmat, axis=1, keepdims=True))
    w_intra = jnp.exp(dmat - mt)
    w_inter = jnp.exp(inter - mt)
    q, k, v = q_ref[...], k_ref[...], v_ref[...]
    s = lax.dot_general(q, k, _NT, preferred_element_type=F32) * w_intra
    for b in range(nseq):
        rows = slice(b * steps, (b + 1) * steps)
        numi_s[rows, :] = lax.dot_general(q[rows, :], c_in[b].astype(BF16), _NT, preferred_element_type=F32)
    num = jnp.dot(s.astype(BF16), v, preferred_element_type=F32) + w_inter * numi_s[...]
    den = (jnp.sum(s, axis=1, keepdims=True)
           + w_inter * jnp.sum(q.astype(F32) * nrow_ref[...], axis=1, keepdims=True))
    hh = num / jnp.maximum(jnp.abs(den), jnp.exp(-mt))
    out_ref[...] = _headnorm_out(hh, mo_ref[...], g_ref[...])
    decay_col = bl_col - b_col + i_col
    decay_row = bl_row - b_row + i_row
    seg_max = jnp.max(jnp.where(same, decay_row, -jnp.inf), axis=1, keepdims=True)
    m_new = jnp.maximum(bl_col + m_col, seg_max)
    ws = jnp.exp(decay_col - m_new)
    cs = jnp.exp(bl_col + m_col - m_new)
    kf = k.astype(F32)
    vw_t = (v.astype(F32) * ws).T.astype(BF16)
    kw = kf * ws
    rid = lax.broadcasted_iota(I32, (R, 1), 0) // steps
    for b in range(nseq):
        r0 = b * steps
        kb = jnp.where(rid == b, kf, 0.0).astype(BF16)
        cs_b = cs[r0:r0 + 1, :]
        c_out[b] = cs_b * c_in[b] + jnp.dot(vw_t, kb, preferred_element_type=F32)
        n_out[b] = cs_b * n_in[b] + jnp.sum(kw[r0:r0 + steps, :], axis=0, keepdims=True)
        m_out[b] = jnp.broadcast_to(m_new[r0:r0 + 1, :], (1, LANES))


def _mlstm_sample(pr, bias, norm_g, state_c, state_n, state_m, nseq_total, steps):
    n = nseq_total * steps
    R = LANES
    spt = R // steps
    m_rows = jnp.pad(jnp.repeat(state_m.astype(F32), steps, axis=0), ((0, 0), (0, LANES - H_M)))
    n_rows = jnp.repeat(state_n.astype(F32).reshape(nseq_total, W_M), steps, axis=0)
    n4 = state_n.astype(F32).reshape(nseq_total, H_M, 1, D_HM)
    colh = lambda w: pl.BlockSpec((R, w), lambda i, h: (i, h))
    sth = lambda *s: pl.BlockSpec((spt, None) + s, lambda i, h: (i, h) + (0,) * len(s))
    sds = jax.ShapeDtypeStruct
    out, c, nn, m = pl.pallas_call(
        functools.partial(_mlstm_sample_kernel, steps=steps),
        grid=(n // R, H_M),
        in_specs=[colh(D_HM), colh(D_HM), colh(D_HM), colh(D_HM),
                  pl.BlockSpec((R, LANES), lambda i, h: (i, 0)),
                  pl.BlockSpec((1, LANES), lambda i, h: (0, 0)),
                  pl.BlockSpec((1, D_HM), lambda i, h: (0, h)),
                  pl.BlockSpec((R, LANES), lambda i, h: (i, 0)),
                  colh(D_HM), sth(D_HM, D_HM), sth(1, D_HM)],
        out_specs=[colh(D_HM), sth(D_HM, D_HM), sth(1, D_HM), sth(1, LANES)],
        out_shape=[sds((n, W_M), BF16), sds((nseq_total, H_M, D_HM, D_HM), F32),
                   sds((nseq_total, H_M, 1, D_HM), F32), sds((nseq_total, H_M, 1, LANES), F32)],
        scratch_shapes=[pltpu.VMEM((R, D_HM), F32)],
        compiler_params=_cparams(("arbitrary", "arbitrary")),
        name="mlstm_sample",
    )(pr["mq"], pr["mk"], pr["mv"], pr["mo"], pr["sm"], bias, norm_g.reshape(1, W_M).astype(F32),
      m_rows, n_rows, state_c.astype(F32), n4)
    return out, c, nn.reshape(nseq_total, H_M, D_HM), m[:, :, 0, 0]


def _rel_bucket(dist):
    max_exact = N_BUCKETS // 2
    d = jnp.maximum(dist, 0)
    df = jnp.maximum(d.astype(F32), 1.0)
    large = max_exact + (jnp.log(df / max_exact) / math.log(MAX_DISTANCE / max_exact)
                         * (N_BUCKETS - max_exact)).astype(I32)
    large = jnp.minimum(large, N_BUCKETS - 1)
    return jnp.where(d < max_exact, d, large)


def _bias_delta(rel_bias, dist):
    far = rel_bias[_rel_bucket(jnp.asarray(4 * MAX_DISTANCE, I32))]
    return (rel_bias[_rel_bucket(dist)] - far).astype(F32)


def _sortable(score):
    score = jnp.where(score == 0.0, 0.0, score)
    bits = pltpu.bitcast(score, I32)
    return bits ^ ((bits >> 31) & 0x7FFFFFFF)


def _key_value(key):
    return pltpu.bitcast(key ^ ((key >> 31) & 0x7FFFFFFF), F32)


def _bracket_search(count_ge, lo_k, c_lo, hi_k, c_hi, active, topk, iters):
    logk = math.log(topk)
    fval = lambda c: jnp.log(c + 0.5) - logk

    def cond(st):
        return jnp.logical_and(st[0] < iters, jnp.max(st[-1]) > 0.5)

    def body(st):
        it, lo_k, hi_k, f_lo, f_hi, c_lo, side, act = st
        lo_v, hi_v = _key_value(lo_k), _key_value(hi_k)
        w = jnp.clip(f_lo / (f_lo - f_hi), 1.0 / 64, 63.0 / 64)
        t_k = _sortable(lo_v + (hi_v - lo_v) * w)
        mid = (lo_k >> 1) + (hi_k >> 1) + (lo_k & hi_k & 1)
        t_k = jnp.where(jnp.logical_or(t_k <= lo_k, t_k >= hi_k), mid, t_k)
        c = count_ge(t_k)
        live = act > 0.5
        up_lo = jnp.logical_and(live, c >= topk)
        up_hi = jnp.logical_and(live, c < topk)
        fn = fval(c)
        f_hi = jnp.where(jnp.logical_and(up_lo, side > 0.5), 0.5 * f_hi, f_hi)
        f_lo = jnp.where(jnp.logical_and(up_hi, side < -0.5), 0.5 * f_lo, f_lo)
        lo_k = jnp.where(up_lo, t_k, lo_k)
        f_lo = jnp.where(up_lo, fn, f_lo)
        c_lo = jnp.where(up_lo, c, c_lo)
        hi_k = jnp.where(up_hi, t_k, hi_k)
        f_hi = jnp.where(up_hi, fn, f_hi)
        side = jnp.where(up_lo, 1.0, jnp.where(up_hi, -1.0, side))
        open_ = jnp.logical_and(c_lo != topk, hi_k > lo_k + 1)
        act = jnp.where(jnp.logical_and(live, open_), 1.0, 0.0)
        return it + 1, lo_k, hi_k, f_lo, f_hi, c_lo, side, act

    open0 = jnp.logical_and(c_lo != topk, hi_k > lo_k + 1)
    act0 = jnp.where(jnp.logical_and(active, open0), 1.0, 0.0)
    init = (jnp.int32(0), lo_k, hi_k, fval(c_lo), fval(c_hi), c_lo, jnp.zeros_like(c_lo), act0)
    st = lax.while_loop(cond, body, init)
    return st[1], st[5], st[-1] > 0.5


def _kth_largest_search(count_ge, total, short, topk, kmin=None, kmax=None):
    def cond(st):
        bit, _, _, done = st
        return jnp.logical_and(bit >= 0, jnp.min(done) < 0.5)

    def body(st):
        bit, pref, cnt, done = st
        cand_u = pref | lax.shift_left(jnp.int32(1), bit)
        c = count_ge(cand_u ^ INT_MIN)
        take = jnp.logical_and(c >= topk, done < 0.5)
        pref = jnp.where(take, cand_u, pref)
        cnt = jnp.where(take, c, cnt)
        done = jnp.where(cnt == topk, 1.0, done)
        return bit - 1, pref, cnt, done

    c_nonneg = count_ge(jnp.zeros(short.shape, I32))
    c_pos = count_ge(jnp.ones(short.shape, I32))
    nonneg = c_nonneg >= topk
    zero_thr = jnp.logical_and(nonneg, c_pos < topk)
    pref0 = jnp.where(nonneg, INT_MIN, 0)
    cnt0 = jnp.where(nonneg, c_nonneg, total)
    done0 = jnp.logical_or(jnp.logical_or(short, zero_thr), cnt0 == topk)
    if kmin is not None:
        lo_k = jnp.where(nonneg, 1, kmin)
        hi_k = jnp.where(nonneg, kmax + 1, 0)
        c_lo = jnp.where(nonneg, c_pos, total)
        c_hi = jnp.where(nonneg, 0.0, c_nonneg)
        lo_k, c_lo, left_open = _bracket_search(count_ge, lo_k, c_lo, hi_k, c_hi, jnp.logical_not(done0),
                                                topk, BRACKET_ITERS)
        found = jnp.logical_and(jnp.logical_not(done0), jnp.logical_not(left_open))
        done0 = jnp.logical_or(done0, found)
    init = (jnp.int32(30), pref0, cnt0, jnp.where(done0, 1.0, 0.0))
    _, pref, cnt, _ = lax.while_loop(cond, body, init)
    thr = pref ^ INT_MIN
    if kmin is not None:
        thr = jnp.where(found, lo_k, thr)
        cnt = jnp.where(found, c_lo, cnt)
    thr = jnp.where(short, INT_MIN + 1, thr)
    return thr, jnp.logical_and(jnp.logical_not(short), cnt != topk)


def _dsa_prompt_kernel(qs_ref, lo_ref, hi_ref, aq_ref, ikt_ref, kt_ref, v_ref, dz_ref, dzmax_ref, o_ref,
                       buf_ref, q2_ref, bnd_ref, kmax_ref, m_ref, l_ref, ls2_ref, acc2_ref, *, tq, topk):
    CH = KEY_CHUNK
    q0 = pl.program_id(1) * tq
    nchunk = jnp.maximum(lax.shift_right_logical(q0 + tq + 2 * CH - 1, 10), 1) * 2

    @pl.when(pl.program_id(1) == 0)
    def _():
        def body(c, mx):
            kc = kt_ref[:, pl.ds(pl.multiple_of(c * CH, CH), CH)].astype(F32)
            sq = kc * kc
            parts = [jnp.sum(sq[h * D_HA:(h + 1) * D_HA], axis=0, keepdims=True) for h in range(H_A)]
            return jnp.maximum(mx, jnp.concatenate(parts, axis=0))
        mx = lax.fori_loop(0, kt_ref.shape[1] // CH, body, jnp.zeros((H_A, CH), F32))
        km = jnp.sqrt(jnp.max(mx, axis=1, keepdims=True))
        for h in range(H_A):
            kmax_ref[h] = jnp.broadcast_to(km[h:h + 1, :], (1, LANES))

    qs = qs_ref[...]
    lo = lo_ref[...]
    hi = hi_ref[...]
    qs_hm = jnp.concatenate([qs[:, h * D_I:(h + 1) * D_I] for h in range(H_I)], axis=0)
    lo_hm = jnp.concatenate([lo[:, SMALL_IW + h:SMALL_IW + h + 1] for h in range(H_I)], axis=0)
    hi_hm = jnp.concatenate([hi[:, SMALL_IW + h:SMALL_IW + h + 1] for h in range(H_I)], axis=0)
    qpos = q0 + lax.broadcasted_iota(I32, (tq, CH), 0)
    lane_ch = lax.broadcasted_iota(I32, (tq, CH), 1)

    def fold(op, x):
        return functools.reduce(op, [x[:, g * LANES:(g + 1) * LANES] for g in range(x.shape[1] // LANES)])

    def score_chunk(c, carry, causal):
        kmax_p, kmin_p = carry
        st = pl.multiple_of(c * CH, CH)
        a = jnp.dot(qs_hm, ikt_ref[:, pl.ds(st, CH)], preferred_element_type=F32)
        t = jnp.maximum(jnp.minimum(a, hi_hm), lo_hm)
        sc = sum(t[h * tq:(h + 1) * tq] for h in range(H_I))
        key = _sortable(sc)
        if not causal:
            buf_ref[:, pl.ds(st, CH)] = key
            return (jnp.maximum(kmax_p, fold(jnp.maximum, key)), jnp.minimum(kmin_p, fold(jnp.minimum, key)))
        valid = st + lane_ch <= qpos
        buf_ref[:, pl.ds(st, CH)] = jnp.where(valid, key, INT_MIN)
        return (jnp.maximum(kmax_p, fold(jnp.maximum, jnp.where(valid, key, KEY_NEG_INF))),
                jnp.minimum(kmin_p, fold(jnp.minimum, jnp.where(valid, key, KEY_POS_INF))))

    n_before = lax.shift_right_logical(q0, 9)
    carry = (jnp.full((tq, LANES), KEY_NEG_INF, I32), jnp.full((tq, LANES), KEY_POS_INF, I32))
    carry = lax.fori_loop(0, n_before, functools.partial(score_chunk, causal=False), carry)
    kmax_p, kmin_p = lax.fori_loop(n_before, nchunk, functools.partial(score_chunk, causal=True), carry)
    kmax = _sortable(jnp.max(_key_value(kmax_p), axis=1, keepdims=True))
    kmin = _sortable(jnp.min(_key_value(kmin_p), axis=1, keepdims=True))

    def count_ge(cand):
        candb = jnp.broadcast_to(cand, (tq, LANES))

        def body(c, acc):
            st = c * (2 * CH)
            for g in range(2 * CH // LANES):
                blk = buf_ref[:, pl.ds(pl.multiple_of(st + g * LANES, LANES), LANES)]
                acc = acc + jnp.where(blk >= candb, 1.0, 0.0)
            return acc

        acc = lax.fori_loop(0, nchunk // 2, body, jnp.zeros((tq, LANES), F32))
        return jnp.sum(acc, axis=1, keepdims=True)

    nvalid = q0 + 1 + lax.broadcasted_iota(I32, (tq, 1), 0)
    short = nvalid <= topk
    thr, tie = _kth_largest_search(count_ge, (nchunk * CH).astype(F32), short, topk, kmin, kmax)
    any_tie = jnp.max(jnp.where(tie, 1.0, 0.0)) > 0.5

    @pl.when(jnp.logical_not(any_tie))
    def _():
        def fin(c, carry):
            st = pl.multiple_of(c * (2 * CH), 2 * CH)
            blk = buf_ref[:, pl.ds(st, 2 * CH)]
            buf_ref[:, pl.ds(st, 2 * CH)] = jnp.where(blk >= thr, 0, NEG_BITS)
            return carry
        lax.fori_loop(0, nchunk // 2, fin, 0)

    @pl.when(any_tie)
    def _():
        need = jnp.broadcast_to(topk - count_ge(thr + 1), (tq, LANES))
        thr_b = jnp.broadcast_to(thr, (tq, LANES))
        r_i = lax.broadcasted_iota(I32, (LANES, LANES), 0)
        c_i = lax.broadcasted_iota(I32, (LANES, LANES), 1)
        upper = jnp.where(r_i < c_i, 1.0, 0.0).astype(BF16)
        ones = jnp.ones((LANES, LANES), BF16)
        groups = 2 * CH // LANES

        def fin(c, seen):
            sts = [pl.multiple_of(c * (2 * CH) + g * LANES, LANES) for g in range(groups)]
            blks = [buf_ref[:, pl.ds(st, LANES)] for st in sts]
            eqs = [jnp.where(b == thr_b, 1.0, 0.0).astype(BF16) for b in blks]
            pres = [jnp.dot(e, upper, preferred_element_type=F32) for e in eqs]
            tots = [jnp.dot(e, ones, preferred_element_type=F32) for e in eqs]
            for g in range(groups):
                sel = jnp.logical_or(blks[g] > thr_b,
                                     jnp.logical_and(blks[g] == thr_b, pres[g] + seen < need))
                buf_ref[:, pl.ds(sts[g], LANES)] = jnp.where(sel, 0, NEG_BITS)
                seen = seen + tots[g]
            return seen
        lax.fori_loop(0, nchunk // 2, fin, jnp.zeros((tq, LANES), F32))

    lane = lax.broadcasted_iota(I32, (tq, LANES), 1)
    left = lane < D_HA
    for hp in range(H_A // 2):
        aqp = aq_ref[:, hp * LANES:(hp + 1) * LANES]
        zero = jnp.zeros_like(aqp)
        q2_ref[hp, 0:tq, :] = jnp.where(left, aqp, zero)
        q2_ref[hp, tq:2 * tq, :] = jnp.where(left, zero, aqp)
    ts = jnp.maximum(q0 - CH, 0)
    nfar = lax.shift_right_logical(ts + CH - 1, 9)
    lim = nfar * CH - ts
    covered = jnp.where(lax.broadcasted_iota(I32, (tq, TAIL), 1) < lim, NEG, 0.0)
    dz_start = pl.multiple_of(CH - (q0 - ts), LANES)
    two = lambda a: jnp.concatenate([a, a], axis=0)

    def logits(hp, st, width, mb2, with_dz):
        kt = kt_ref[hp * LANES:(hp + 1) * LANES, pl.ds(st, width)]
        s = jnp.dot(q2_ref[hp], kt, preferred_element_type=F32) + mb2
        if with_dz:
            s = s + jnp.concatenate([dz_ref[2 * hp, :, pl.ds(dz_start, width)],
                                     dz_ref[2 * hp + 1, :, pl.ds(dz_start, width)]], axis=0)
        return s

    def mask_bias(st, width, extra):
        mb = pltpu.bitcast(buf_ref[:, pl.ds(st, width)], F32)
        return two(mb if extra is None else mb + extra)

    def lane_groups(p):
        return sum(p[:, g * LANES:(g + 1) * LANES] for g in range(p.shape[1] // LANES))

    for hp in range(H_A // 2):
        q2 = q2_ref[hp].astype(F32)
        qn = jnp.sqrt(jnp.sum(q2 * q2, axis=1, keepdims=True))
        r2 = lax.broadcasted_iota(I32, (2 * tq, 1), 0)
        kmax = jnp.where(r2 < tq, kmax_ref[2 * hp][:, 0:1], kmax_ref[2 * hp + 1][:, 0:1])
        dzmax = jnp.where(r2 < tq, dzmax_ref[2 * hp][:, 0:1], dzmax_ref[2 * hp + 1][:, 0:1])
        bnd_ref[hp] = qn * kmax + dzmax
    acc2_ref[...] = jnp.zeros(acc2_ref.shape, F32)
    ls2_ref[...] = jnp.zeros(ls2_ref.shape, F32)

    def attend_fast(st, width, extra, with_dz):
        mb2 = mask_bias(st, width, extra)
        for hp in range(H_A // 2):
            p = jnp.exp2(logits(hp, st, width, mb2, with_dz) - bnd_ref[hp])
            ls2_ref[hp] = ls2_ref[hp] + lane_groups(p)
            vv = v_ref[pl.ds(st, width), hp * LANES:(hp + 1) * LANES]
            acc2_ref[hp] = acc2_ref[hp] + jnp.dot(p.astype(BF16), vv, preferred_element_type=F32)

    def far_fast(jc, carry):
        attend_fast(pl.multiple_of(jc * 2 * CH, 2 * CH), 2 * CH, None, False)
        return carry

    lax.fori_loop(0, nfar // 2, far_fast, 0)

    @pl.when(nfar % 2 == 1)
    def _():
        attend_fast(pl.multiple_of((nfar - 1) * CH, CH), CH, None, False)
    attend_fast(pl.multiple_of(ts, LANES), TAIL, covered, True)
    lmin = jnp.float32(jnp.inf)
    for hp in range(H_A // 2):
        l2 = jnp.sum(ls2_ref[hp], axis=1, keepdims=True)
        lmin = jnp.minimum(lmin, jnp.min(l2))
        acc = acc2_ref[hp]
        o_ref[:, hp * LANES:(hp + 1) * LANES] = jnp.where(left, acc[0:tq] / l2[0:tq], acc[tq:2 * tq] / l2[tq:2 * tq])
    underflow = jnp.logical_not(lmin > 1e-30)

    @pl.when(underflow)
    def _():
        m_ref[...] = jnp.full(m_ref.shape, NEG, F32)
        l_ref[...] = jnp.zeros(l_ref.shape, F32)
        acc2_ref[...] = jnp.zeros(acc2_ref.shape, F32)

        def attend_exact(st, width, extra, with_dz):
            mb2 = mask_bias(st, width, extra)
            for hp in range(H_A // 2):
                s = logits(hp, st, width, mb2, with_dz)
                m_old = m_ref[hp]
                m_new = jnp.maximum(m_old, jnp.max(s, axis=1, keepdims=True))
                alpha = jnp.exp2(m_old - m_new)
                p = jnp.exp2(s - m_new)
                l_ref[hp] = alpha * l_ref[hp] + jnp.sum(p, axis=1, keepdims=True)
                m_ref[hp] = m_new
                vv = v_ref[pl.ds(st, width), hp * LANES:(hp + 1) * LANES]
                acc2_ref[hp] = alpha * acc2_ref[hp] + jnp.dot(p.astype(BF16), vv, preferred_element_type=F32)

        def far_exact(jc, carry):
            attend_exact(pl.multiple_of(jc * CH, CH), CH, None, False)
            return carry

        lax.fori_loop(0, nfar, far_exact, 0)
        attend_exact(pl.multiple_of(ts, LANES), TAIL, covered, True)
        for hp in range(H_A // 2):
            acc = acc2_ref[hp]
            l2 = l_ref[hp]
            o_ref[:, hp * LANES:(hp + 1) * LANES] = jnp.where(left, acc[0:tq] / l2[0:tq],
                                                              acc[tq:2 * tq] / l2[tq:2 * tq])


def _dsa_prompt(pr, rel_bias, nb, seq, tq):
    topk = min(TOPK_MAX, seq // 4)
    assert seq % (2 * KEY_CHUNK) == 0 and seq >= TAIL + KEY_CHUNK and tq == LANES
    wtab = KEY_CHUNK + TAIL
    dmin, dmax = KEY_CHUNK - (wtab - 1), KEY_CHUNK + tq - 1
    g = (_bias_delta(rel_bias, jnp.arange(dmax, dmin - 1, -1, dtype=I32)) * LOG2E).T
    dz = jnp.stack([g[:, tq - 1 - r:tq - 1 - r + wtab] for r in range(tq)], axis=1)
    dzmax = jnp.broadcast_to(jnp.max(g, axis=1).reshape(H_A, 1, 1), (H_A, 1, LANES))
    nq = seq // tq
    rowq = lambda w: pl.BlockSpec((tq, w), lambda b, i: (b * nq + i, 0))
    perb = lambda r, c: pl.BlockSpec((None, r, c), lambda b, i: (b, 0, 0), pipeline_mode=pl.Buffered(1))
    vb = pr["avb"].reshape(nb, seq, W_A)
    return pl.pallas_call(
        functools.partial(_dsa_prompt_kernel, tq=tq, topk=topk),
        grid=(nb, nq),
        in_specs=[rowq(H_I * D_I), rowq(LANES), rowq(LANES), rowq(W_A),
                  perb(D_I, seq), perb(W_A, seq), perb(seq, W_A), _const_spec((H_A, tq, wtab)),
                  _const_spec((H_A, 1, LANES))],
        out_specs=rowq(W_A),
        out_shape=jax.ShapeDtypeStruct((nb * seq, W_A), F32),
        scratch_shapes=[pltpu.VMEM((tq, seq), I32),
                        pltpu.VMEM((H_A // 2, 2 * tq, LANES), BF16),
                        pltpu.VMEM((H_A // 2, 2 * tq, 1), F32),
                        pltpu.VMEM((H_A, 1, LANES), F32),
                        pltpu.VMEM((H_A // 2, 2 * tq, 1), F32), pltpu.VMEM((H_A // 2, 2 * tq, 1), F32),
                        pltpu.VMEM((H_A // 2, 2 * tq, LANES), F32),
                        pltpu.VMEM((H_A // 2, 2 * tq, LANES), F32)],
        compiler_params=_cparams(("arbitrary", "arbitrary")),
        name="dsa_prompt",
    )(pr["qs"], pr["lo"], pr["hi"], pr["aq"], pr["iktb"], pr["ktb"], vb, dz, dzmax)


def _dsa_sample_kernel(pt_ref, qs_ref, lo_ref, hi_ref, aq_ref, ikn_ref, akn_ref, avn_ref, dz_ref,
                       cki_ref, ck_ref, cv_ref, o_ref,
                       kib, kbuf, vbuf, buf_ref, sem_ki, sem_kv, *, steps, topk, n_pages, page, cpp):
    b = pl.program_id(0)
    nb = pl.num_programs(0)
    past = n_pages * page
    nch = n_pages // cpp
    ckeys = cpp * page
    width = past + LANES
    rows_a = H_A * steps

    def ki_copy(seq_i, slot, p):
        return pltpu.make_async_copy(cki_ref.at[pt_ref[seq_i, p]], kib.at[slot, p], sem_ki.at[slot])

    def kv_copies(seq_i, c, slot, p):
        pg = pt_ref[seq_i, c * cpp + p]
        return (pltpu.make_async_copy(ck_ref.at[pg], kbuf.at[slot, p], sem_kv.at[slot]),
                pltpu.make_async_copy(cv_ref.at[pg], vbuf.at[slot, p], sem_kv.at[slot]))

    def start_ki(seq_i, slot):
        def f(p, carry):
            ki_copy(seq_i, slot, p).start()
            return carry
        lax.fori_loop(0, n_pages, f, 0)

    n_slots = kbuf.shape[0]
    ahead = n_slots - 1
    total_chunks = nb * nch

    def start_kv(g):
        @pl.when(g < total_chunks)
        def _():
            def f(p, carry):
                for cp in kv_copies(g // nch, g % nch, g % n_slots, p):
                    cp.start()
                return carry
            lax.fori_loop(0, cpp, f, 0)

    def wait_kv(g):
        def f(p, carry):
            for cp in kv_copies(g // nch, g % nch, g % n_slots, p):
                cp.wait()
            return carry
        lax.fori_loop(0, cpp, f, 0)

    slot_b = b % 2

    @pl.when(b == 0)
    def _():
        start_ki(0, 0)
        for g0 in range(ahead):
            start_kv(jnp.int32(g0))

    def wait_ki(p, carry):
        ki_copy(b, slot_b, p).wait()
        return carry
    lax.fori_loop(0, n_pages, wait_ki, 0)

    @pl.when(b + 1 < nb)
    def _():
        start_ki(b + 1, 1 - slot_b)

    qs = qs_ref[...]
    lo = lo_ref[...]
    hi = hi_ref[...]
    qs_hm = jnp.concatenate([qs[:, h * D_I:(h + 1) * D_I] for h in range(H_I)], axis=0)
    lo_hm = jnp.concatenate([lo[:, SMALL_IW + h:SMALL_IW + h + 1] for h in range(H_I)], axis=0)
    hi_hm = jnp.concatenate([hi[:, SMALL_IW + h:SMALL_IW + h + 1] for h in range(H_I)], axis=0)

    def head_sum(a):
        t = jnp.maximum(jnp.minimum(a, hi_hm), lo_hm)
        return sum(t[h * steps:(h + 1) * steps] for h in range(H_I))

    def score_page(p, carry):
        a = jnp.dot(qs_hm, kib[slot_b, p].astype(BF16), preferred_element_type=F32)
        buf_ref[:, pl.ds(pl.multiple_of(p * page, page), page)] = _sortable(head_sum(a))
        return carry
    lax.fori_loop(0, n_pages, score_page, 0, unroll=4)

    ikn = jnp.concatenate([ikn_ref[...], jnp.zeros((LANES - steps, D_I), F32)], axis=0).astype(BF16)
    a_new = lax.dot_general(qs_hm, ikn, _NT, preferred_element_type=F32)
    t_i = lax.broadcasted_iota(I32, (steps, LANES), 0)
    j_i = lax.broadcasted_iota(I32, (steps, LANES), 1)
    buf_ref[:, past:width] = jnp.where(j_i <= t_i, _sortable(head_sum(a_new)), INT_MIN)

    def count_ge(cand):
        return jnp.sum(jnp.where(buf_ref[...] >= cand, 1.0, 0.0), axis=1, keepdims=True)

    nvalid = past + 1 + lax.broadcasted_iota(I32, (steps, 1), 0)
    short = nvalid <= topk
    keys = buf_ref[...]
    kmax = _sortable(jnp.max(_key_value(jnp.maximum(keys, KEY_NEG_INF)), axis=1, keepdims=True))
    kmin = _sortable(jnp.min(_key_value(jnp.where(keys == INT_MIN, KEY_POS_INF, keys)), axis=1, keepdims=True))
    thr, tie = _kth_largest_search(count_ge, jnp.float32(width), short, topk, kmin, kmax)
    any_tie = jnp.max(jnp.where(tie, 1.0, 0.0)) > 0.5

    @pl.when(jnp.logical_not(any_tie))
    def _():
        buf_ref[...] = jnp.where(buf_ref[...] >= thr, 0, NEG_BITS)

    @pl.when(any_tie)
    def _():
        need = jnp.broadcast_to(topk - count_ge(thr + 1), (steps, LANES))
        thr_b = jnp.broadcast_to(thr, (steps, LANES))
        r_i = lax.broadcasted_iota(I32, (LANES, LANES), 0)
        c_i = lax.broadcasted_iota(I32, (LANES, LANES), 1)
        upper = jnp.where(r_i < c_i, 1.0, 0.0).astype(BF16)
        ones = jnp.ones((LANES, LANES), BF16)
        cols = [slice(g * LANES, (g + 1) * LANES) for g in range(width // LANES)]
        blks = [buf_ref[:, cs] for cs in cols]
        eqs = [jnp.where(b == thr_b, 1.0, 0.0).astype(BF16) for b in blks]
        pres = [jnp.dot(e, upper, preferred_element_type=F32) for e in eqs]
        tots = [jnp.dot(e, ones, preferred_element_type=F32) for e in eqs]
        seen = jnp.zeros((steps, LANES), F32)
        for g, cs in enumerate(cols):
            sel = jnp.logical_or(blks[g] > thr_b, jnp.logical_and(blks[g] == thr_b, pres[g] + seen < need))
            buf_ref[:, cs] = jnp.where(sel, 0, NEG_BITS)
            seen = seen + tots[g]

    r_head = lax.broadcasted_iota(I32, (rows_a, W_A), 0) // steps
    c_head = lax.broadcasted_iota(I32, (rows_a, W_A), 1) // D_HA
    own = r_head == c_head
    aq_t = jnp.concatenate([aq_ref[...]] * H_A, axis=0)
    q_bd = jnp.where(own, aq_t, jnp.zeros_like(aq_t))
    tile_rows = lambda a: jnp.concatenate([a] * H_A, axis=0)

    def softmax_step(state, s, pv_fn):
        m_old, l_old, acc = state
        m_new = jnp.maximum(m_old, jnp.max(s, axis=1, keepdims=True))
        alpha = jnp.exp(m_old - m_new)
        p = jnp.exp(s - m_new)
        return m_new, alpha * l_old + jnp.sum(p, axis=1, keepdims=True), alpha * acc + pv_fn(p.astype(BF16))

    def chunk_step(c, state, last):
        g = b * nch + c
        slot = g % n_slots
        wait_kv(g)
        start_kv(g + ahead)

        s = jnp.concatenate([jnp.dot(q_bd, kbuf[slot, p].astype(BF16), preferred_element_type=F32)
                             for p in range(cpp)], axis=1)
        s = s + tile_rows(pltpu.bitcast(buf_ref[:, pl.ds(pl.multiple_of(c * ckeys, ckeys), ckeys)], F32))
        if last:
            s = s + jnp.concatenate([jnp.zeros((rows_a, ckeys - page), F32), dz_ref[:, 0:page]], axis=1)

        def pv(pb):
            return sum(lax.dot_general(pb[:, p * page:(p + 1) * page], vbuf[slot, p].astype(BF16), _NT,
                                       preferred_element_type=F32) for p in range(cpp))
        return softmax_step(state, s, pv)

    state = (jnp.full((rows_a, 1), NEG, F32), jnp.zeros((rows_a, 1), F32), jnp.zeros((rows_a, W_A), F32))
    state = lax.fori_loop(0, nch - 1, lambda c, st: chunk_step(c, st, False), state)
    state = chunk_step(nch - 1, state, True)
    pad = jnp.zeros((LANES - steps, W_A), F32)
    kn = jnp.concatenate([akn_ref[...], pad], axis=0).astype(BF16)
    vn = jnp.concatenate([avn_ref[...], pad], axis=0).astype(BF16)
    s_new = (lax.dot_general(q_bd, kn, _NT, preferred_element_type=F32)
             + tile_rows(pltpu.bitcast(buf_ref[:, past:width], F32)) + dz_ref[:, page:2 * page])
    _, l_fin, acc = softmax_step(state, s_new, lambda pb: jnp.dot(pb, vn, preferred_element_type=F32))
    accn = jnp.where(own, acc / l_fin, 0.0)
    o_ref[...] = sum(accn[h * steps:(h + 1) * steps] for h in range(H_A))


def _dsa_sample(pr, cache_k, cache_v, cache_k_idx, page_table, rel_bias, nseq, steps):
    n_pool, page = cache_k.shape[0], cache_k.shape[1]
    n_pages = page_table.shape[1]
    past = n_pages * page
    topk = min(TOPK_MAX, (past + steps) // 4)
    cpp = 8 if n_pages % 8 == 0 else n_pages
    assert page == LANES and steps <= LANES
    t = jnp.arange(steps, dtype=I32)[:, None]
    c = jnp.arange(LANES, dtype=I32)[None, :]
    dz = jnp.transpose(_bias_delta(rel_bias, jnp.concatenate([t + LANES - c, t - c], axis=1)), (2, 0, 1))
    dz = dz.reshape(H_A * steps, 2 * LANES)
    ck_t = jnp.transpose(cache_k, (0, 2, 3, 1)).reshape(n_pool, W_A, page)
    cv_t = jnp.transpose(cache_v, (0, 2, 3, 1)).reshape(n_pool, W_A, page)
    cki_t = jnp.transpose(cache_k_idx, (0, 2, 1))
    rows = lambda w: pl.BlockSpec((steps, w), lambda b, pt: (b, 0))
    anyspec = pl.BlockSpec(memory_space=pl.ANY)
    return pl.pallas_call(
        functools.partial(_dsa_sample_kernel, steps=steps, topk=topk, n_pages=n_pages, page=page, cpp=cpp),
        grid_spec=pltpu.PrefetchScalarGridSpec(
            num_scalar_prefetch=1,
            grid=(nseq,),
            in_specs=[rows(H_I * D_I), rows(LANES), rows(LANES), rows(W_A), rows(D_I), rows(W_A), rows(W_A),
                      pl.BlockSpec((H_A * steps, 2 * LANES), lambda b, pt: (0, 0)),
                      anyspec, anyspec, anyspec],
            out_specs=rows(W_A),
            scratch_shapes=[pltpu.VMEM((2, n_pages, D_I, page), F32),
                            pltpu.VMEM((KV_SLOTS, cpp, W_A, page), F32), pltpu.VMEM((KV_SLOTS, cpp, W_A, page), F32),
                            pltpu.VMEM((steps, past + LANES), I32),
                            pltpu.SemaphoreType.DMA((2,)), pltpu.SemaphoreType.DMA((KV_SLOTS,))],
        ),
        out_shape=jax.ShapeDtypeStruct((nseq * steps, W_A), F32),
        compiler_params=_cparams(("arbitrary",)),
        name="dsa_sample",
    )(page_table.astype(I32), pr["qs"], pr["lo"], pr["hi"], pr["aq"], pr["ik"], pr["ak"], pr["av"], dz,
      cki_t, ck_t, cv_t)


def _layer_norm(y, g, b):
    mu = jnp.mean(y, axis=1, keepdims=True)
    var = jnp.mean(jnp.square(y - mu), axis=1, keepdims=True)
    return (y - mu) * lax.rsqrt(var + LN_EPS) * g + b


def _mix_router_kernel(x_ref, mo_ref, ao_ref, wo1_ref, wo2_ref, g_ref, b_ref, rw_ref, rb_ref,
                       h_ref, route_ref, gate_ref, cnt_ref, carry_s):
    @pl.when(pl.program_id(0) == 0)
    def _():
        carry_s[...] = jnp.zeros_like(carry_s)

    tm = x_ref.shape[0]
    mix = (jnp.dot(mo_ref[...], wo1_ref[...], preferred_element_type=F32)
           + jnp.dot(ao_ref[...].astype(BF16), wo2_ref[...], preferred_element_type=F32))
    h = _layer_norm(DN_ALPHA * x_ref[...] + mix, g_ref[...], b_ref[...])
    h_ref[...] = h
    logits = jnp.dot(h.astype(BF16), rw_ref[...], preferred_element_type=F32) + rb_ref[...]
    lane = lax.broadcasted_iota(I32, (tm, LANES), 1)
    lane_f = lane.astype(F32)
    l = jnp.where(lane < N_EXPERTS, logits, -jnp.inf)
    vals, hots, idxs = [], [], []
    member = jnp.zeros((tm, LANES), F32)
    for _ in range(TOP_K_EXP):
        mx = jnp.max(l, axis=1, keepdims=True)
        idx = jnp.min(jnp.where(l == mx, lane_f, float(LANES)), axis=1, keepdims=True)
        hot = lane_f == idx
        vals.append(mx)
        idxs.append(idx)
        hots.append(hot)
        member = member + jnp.where(hot, 1.0, 0.0)
        l = jnp.where(hot, -jnp.inf, l)
    ex = [jnp.exp(v - vals[0]) for v in vals]
    tot = sum(ex)
    stril = (lax.broadcasted_iota(I32, (tm, tm), 1) < lax.broadcasted_iota(I32, (tm, tm), 0))
    before = jnp.dot(jnp.where(stril, 1.0, 0.0).astype(BF16), member.astype(BF16),
                     preferred_element_type=F32) + carry_s[...]
    route = jnp.zeros((tm, LANES), F32)
    gates = jnp.zeros((tm, LANES), F32)
    for k in range(TOP_K_EXP):
        rank = jnp.sum(jnp.where(hots[k], before, 0.0), axis=1, keepdims=True)
        route = route + jnp.where(lane == k, idxs[k], 0.0) + jnp.where(lane == TOP_K_EXP + k, rank, 0.0)
        gates = gates + jnp.where(lane == k, ex[k] / tot, 0.0)
    route_ref[...] = route.astype(I32)
    gate_ref[...] = gates
    carry_s[...] = carry_s[...] + jnp.sum(member, axis=0, keepdims=True)
    cnt_ref[...] = carry_s[...]


def _scatter_rows_kernel(zstart_ref, dest_ref, h_ref, xs_out, zbuf, sem, zsem):
    ts = h_ref.shape[0]

    @pl.when(pl.program_id(0) == 0)
    def _():
        zbuf[...] = jnp.zeros_like(zbuf)

        def zero_block(j, go):
            @pl.when(zstart_ref[j] >= 0)
            def _():
                rows = pl.ds(pl.multiple_of(zstart_ref[j], ROW_BLOCK), ROW_BLOCK)
                go(pltpu.make_async_copy(zbuf, xs_out.at[rows], zsem))
        for j in range(2 * N_EXPERTS):
            zero_block(j, lambda cp: cp.start())
        for j in range(2 * N_EXPERTS):
            zero_block(j, lambda cp: cp.wait())

    def start(t, carry):
        for k in range(TOP_K_EXP):
            d = dest_ref[t * TOP_K_EXP + k]
            pltpu.make_async_copy(h_ref.at[pl.ds(t, 1)], xs_out.at[pl.ds(d, 1)], sem).start(priority=k % 2)
        return carry
    lax.fori_loop(0, ts, start, 0)

    def wait(t, carry):
        for k in range(TOP_K_EXP):
            pltpu.make_async_copy(h_ref.at[pl.ds(0, 1)], xs_out.at[pl.ds(0, 1)], sem).wait()
        return carry
    lax.fori_loop(0, ts, wait, 0)


def _expert_kernel(be_ref, na_ref, xs_ref, wgu_ref, bgu_ref, wdn_ref, bdn_ref, ys_ref):
    del be_ref
    active = pl.program_id(0) < na_ref[0]

    @pl.when(jnp.logical_not(active))
    def _():
        ys_ref[...] = jnp.zeros_like(ys_ref)

    @pl.when(active)
    def _():
        f = wdn_ref.shape[0]
        gu = jnp.dot(xs_ref[...].astype(BF16), wgu_ref[...], preferred_element_type=F32) + bgu_ref[...]
        gt = jnp.minimum(gu[:, :f], SWIGLU_LIMIT)
        up = jnp.clip(gu[:, f:], -SWIGLU_LIMIT, SWIGLU_LIMIT)
        act = (up + 1.0) * gt / (1.0 + jnp.exp(-SWIGLU_ALPHA * gt))
        ys_ref[...] = jnp.dot(act.astype(BF16), wdn_ref[...], preferred_element_type=F32) + bdn_ref[...]


def _combine_kernel(dest_ref, dest_next_ref, gate_ref, h_ref, g_ref, b_ref, ys_ref, o_ref, ybuf, sem):
    tc = h_ref.shape[0]
    i = pl.program_id(0)
    slot = i % 2

    def copy(s, t, k, d):
        return pltpu.make_async_copy(ys_ref.at[pl.ds(d, 1)], ybuf.at[s, k, pl.ds(t, 1)], sem.at[s])

    def start_tile(idx_ref, s):
        def start(t, carry):
            for k in range(TOP_K_EXP):
                copy(s, t, k, idx_ref[t * TOP_K_EXP + k]).start(priority=k % 2)
            return carry
        lax.fori_loop(0, tc, start, 0)

    @pl.when(i == 0)
    def _():
        start_tile(dest_ref, 0)

    @pl.when(i + 1 < pl.num_programs(0))
    def _():
        start_tile(dest_next_ref, 1 - slot)

    def wait(t, carry):
        for k in range(TOP_K_EXP):
            copy(slot, 0, k, 0).wait()
        return carry
    lax.fori_loop(0, tc, wait, 0)
    gates = gate_ref[...]
    y = sum(gates[:, k:k + 1] * ybuf[slot, k] for k in range(TOP_K_EXP))
    o_ref[...] = _layer_norm(DN_ALPHA * h_ref[...] + y, g_ref[...], b_ref[...])


def _finish(x2, m_out, a_out, ffn, tm, tr):
    n, d = x2.shape
    wo1, wo2, ln1_g, ln1_b, rw, rb, wgu, bgu, wdn, bdn, ln2_g, ln2_b = ffn
    f = wdn.shape[1]
    vec = lambda a: a.reshape(1, -1).astype(F32)
    row = lambda t, w: pl.BlockSpec((t, w), lambda i, *_: (i, 0))
    sds = jax.ShapeDtypeStruct
    h, route, gates, cnt = pl.pallas_call(
        _mix_router_kernel,
        grid=(n // tm,),
        in_specs=[row(tm, d), row(tm, W_M), row(tm, W_A), _const_spec(wo1.shape), _const_spec(wo2.shape),
                  _const_spec((1, d)), _const_spec((1, d)), _const_spec(rw.shape), _const_spec((1, LANES))],
        out_specs=[row(tm, d), row(tm, LANES), row(tm, LANES), pl.BlockSpec((1, LANES), lambda i: (0, 0))],
        out_shape=[sds((n, d), F32), sds((n, LANES), I32), sds((n, LANES), F32), sds((1, LANES), F32)],
        scratch_shapes=[pltpu.VMEM((1, LANES), F32)],
        compiler_params=_cparams(("arbitrary",)),
        name="mix_router",
    )(x2, m_out, a_out, wo1, wo2, vec(ln1_g), vec(ln1_b), rw, rb)

    counts = cnt[0, :N_EXPERTS].astype(I32)
    padded = (counts + ROW_BLOCK - 1) // ROW_BLOCK * ROW_BLOCK
    pad_end = jnp.cumsum(padded)
    pad_start = pad_end - padded
    dest = (pad_start[route[:, :TOP_K_EXP]] + route[:, TOP_K_EXP:2 * TOP_K_EXP]).reshape(-1)
    n_blk = -(-n * TOP_K_EXP // ROW_BLOCK) + N_EXPERTS
    blk_start = jnp.arange(n_blk, dtype=I32) * ROW_BLOCK
    blk_e = jnp.minimum(jnp.sum((pad_end[None, :] <= blk_start[:, None]).astype(I32), axis=1), N_EXPERTS - 1)
    n_act = (pad_end[-1:] // ROW_BLOCK).astype(I32)
    tail_rows = pad_end[-1] + jnp.arange(N_EXPERTS, dtype=I32) * ROW_BLOCK
    zero_blocks = jnp.concatenate([jnp.where(counts > 0, pad_end - ROW_BLOCK, -1),
                                   jnp.where(tail_rows < n_blk * ROW_BLOCK, tail_rows, -1)]).astype(I32)

    n_steps = n // tr
    smem_dest = pl.BlockSpec((tr * TOP_K_EXP,), lambda i, *_: (i,), memory_space=pltpu.SMEM)
    smem_next = pl.BlockSpec((tr * TOP_K_EXP,), lambda i, *_: (jnp.minimum(i + 1, n_steps - 1),),
                             memory_space=pltpu.SMEM)
    anyspec = pl.BlockSpec(memory_space=pl.ANY)
    xs = pl.pallas_call(
        _scatter_rows_kernel,
        grid_spec=pltpu.PrefetchScalarGridSpec(
            num_scalar_prefetch=1,
            grid=(n_steps,),
            in_specs=[smem_dest, row(tr, d)],
            out_specs=anyspec,
            scratch_shapes=[pltpu.VMEM((ROW_BLOCK, d), F32), pltpu.SemaphoreType.DMA(()),
                            pltpu.SemaphoreType.DMA(())],
        ),
        out_shape=sds((n_blk * ROW_BLOCK, d), F32),
        compiler_params=_cparams(("arbitrary",)),
        name="moe_scatter",
    )(zero_blocks, dest, h)

    blk = lambda i, be, na: jnp.minimum(i, na[0] - 1)
    ys = pl.pallas_call(
        _expert_kernel,
        grid_spec=pltpu.PrefetchScalarGridSpec(
            num_scalar_prefetch=2,
            grid=(n_blk,),
            in_specs=[pl.BlockSpec((ROW_BLOCK, d), lambda i, be, na: (blk(i, be, na), 0)),
                      pl.BlockSpec((None, d, 2 * f), lambda i, be, na: (be[i], 0, 0)),
                      pl.BlockSpec((None, 1, 2 * f), lambda i, be, na: (be[i], 0, 0)),
                      pl.BlockSpec((None, f, d), lambda i, be, na: (be[i], 0, 0)),
                      pl.BlockSpec((None, 1, d), lambda i, be, na: (be[i], 0, 0))],
            out_specs=pl.BlockSpec((ROW_BLOCK, d), lambda i, be, na: (i, 0)),
        ),
        out_shape=sds((n_blk * ROW_BLOCK, d), F32),
        compiler_params=_cparams(("arbitrary",)),
        name="moe_experts",
    )(blk_e, n_act, xs, wgu, bgu, wdn, bdn)

    return pl.pallas_call(
        _combine_kernel,
        grid=(n_steps,),
        in_specs=[smem_dest, smem_next, row(tr, LANES), row(tr, d), _const_spec((1, d)), _const_spec((1, d)),
                  anyspec],
        out_specs=row(tr, d),
        out_shape=sds((n, d), F32),
        scratch_shapes=[pltpu.VMEM((2, TOP_K_EXP, tr, d), F32), pltpu.SemaphoreType.DMA((2,))],
        compiler_params=_cparams(("arbitrary",)),
        name="moe_combine",
    )(dest, dest, gates, h, vec(ln2_g), vec(ln2_b), ys)


def _pack_ffn(w_out, ln1_g, ln1_b, router_w, router_b, w_gu, b_gu, w_dn, b_dn, ln2_g, ln2_b):
    d = w_out.shape[1]
    rw = jnp.pad(router_w, ((0, 0), (0, LANES - N_EXPERTS))).astype(BF16)
    rb = jnp.pad(router_b.astype(F32), (0, LANES - N_EXPERTS)).reshape(1, LANES)
    return (w_out[:W_M].astype(BF16), w_out[W_M:].astype(BF16), ln1_g, ln1_b, rw, rb,
            w_gu.astype(BF16), b_gu.astype(F32).reshape(N_EXPERTS, 1, -1),
            w_dn.astype(BF16), b_dn.astype(F32).reshape(N_EXPERTS, 1, d), ln2_g, ln2_b)


def _tile(n, pref):
    t = pref
    while n % t:
        t //= 2
    return t


def kernel(x_prompt, x_sample, cache_k, cache_v, cache_k_idx, state_C, state_n, state_m, page_table,
           w_in, b_ig, b_fg, mlstm_norm_g, w_out, ln1_g, ln1_b, router_w, router_b,
           w_gate_up, b_gate_up, w_down, b_down, ln2_g, ln2_b, rel_bias):
    assert w_in.shape[0] == DEPTH
    nb, seq, d = x_prompt.shape
    ns, steps, _ = x_sample.shape
    bias = _gate_bias(b_ig[0], b_fg[0])
    ffn = _pack_ffn(w_out[0], ln1_g[0], ln1_b[0], router_w[0], router_b[0], w_gate_up[0], b_gate_up[0],
                    w_down[0], b_down[0], ln2_g[0], ln2_b[0])

    xp = x_prompt.reshape(nb * seq, d)
    pp = _project(xp, w_in[0], nb, seq, _tile(seq, 256), True)
    mo_p, c_p, n_p, m_p = _mlstm_prompt(pp, bias, mlstm_norm_g[0], nb, seq, _tile(seq, 2 * LANES))
    ao_p = _dsa_prompt(pp, rel_bias, nb, seq, LANES)
    y_p = _finish(xp, mo_p, ao_p, ffn, _tile(nb * seq, 512), _tile(nb * seq, 256))

    xs = x_sample.reshape(ns * steps, d)
    ps = _project(xs, w_in[0], 1, ns * steps, _tile(ns * steps, 256), False)
    mo_s, c_s, n_s, m_s = _mlstm_sample(ps, bias, mlstm_norm_g[0], state_C[0], state_n[0], state_m[0], ns, steps)
    ao_s = _dsa_sample(ps, cache_k[0], cache_v[0], cache_k_idx[0], page_table, rel_bias, ns, steps)
    y_s = _finish(xs, mo_s, ao_s, ffn, _tile(ns * steps, 512), _tile(ns * steps, 256))

    dt = x_prompt.dtype
    L = lambda a: a[None]
    heads = lambda a: jnp.transpose(a.reshape(nb, H_A, D_HA, seq), (0, 3, 1, 2))
    return (y_p.reshape(nb, seq, d), y_s.reshape(ns, steps, d),
            L(heads(pp["ktf"])), L(heads(pp["vtf"])), L(jnp.transpose(pp["iktf"], (0, 2, 1))),
            L(c_p.astype(dt)), L(n_p.astype(dt)), L(m_p.astype(dt)),
            L(ps["ak"].reshape(ns, steps, H_A, D_HA)), L(ps["av"].reshape(ns, steps, H_A, D_HA)),
            L(ps["ik"].reshape(ns, steps, D_I)),
            L(c_s.astype(state_C.dtype)), L(n_s.astype(state_n.dtype)), L(m_s.astype(state_m.dtype)))
```
